```python
import math
import jax
import jax.numpy as jnp
from jax import lax
import numpy as np

D_MODEL = 2048
BATCH = 4
SEQ = 4096
DEPTH = 2
DEC_BATCH = 16
DEC_SEQ = 16
PAST_LEN = 4096

CHUNK = 64
QUERY_BLOCK = 128
N_MIXERS = 2
N_ATTN_LAYERS = (DEPTH + 1) // 2
N_REC_LAYERS = DEPTH // 2

ATTN_HEADS = 8
ATTN_HEAD_DIM = D_MODEL // ATTN_HEADS // 2
ATTN_V_DIM = 2 * ATTN_HEAD_DIM
ATTN_SCALE = ATTN_HEAD_DIM ** -0.5

REC_EXPAND = 128
REC_HEADS = D_MODEL // REC_EXPAND
REC_DK = REC_EXPAND
REC_DV = D_MODEL // REC_HEADS
REC_WIDTH = REC_HEADS * REC_DK

D_FF = 5632
CONV_W = 3
EPS = 1e-6

kernel_name = "diffattn_hgrn2_convffn_streaming_step"


def rms_norm(x, g):
    x32 = x.astype(jnp.float32)
    y = x32 * lax.rsqrt(jnp.mean(x32 * x32, axis=-1, keepdims=True) + EPS)
    return (y * g.astype(jnp.float32)).astype(x.dtype)


def alibi_slopes():
    h = jnp.arange(1, ATTN_HEADS + 1, dtype=jnp.float32)
    return jnp.exp2(-8.0 * h / ATTN_HEADS)


def diff_attention_core(q, k, v, q_pos, k_pos, lam, slopes):
    s = jnp.einsum('bqhcd,bkhcd->bhcqk', q, k) * ATTN_SCALE
    dist = jnp.abs(q_pos[:, None] - k_pos[None, :]).astype(jnp.float32)
    allowed = (k_pos[None, :] // CHUNK) <= (q_pos[:, None] // CHUNK)
    s = jnp.where(allowed, s - slopes[:, None, None, None] * dist, -jnp.inf)
    p = jax.nn.softmax(s, axis=-1)
    a = p[:, :, 0] - lam * p[:, :, 1]
    return jnp.einsum('bhqk,bkhe->bqhe', a, v)


def diff_attention(h, k_past, v_past, w_qkv, lq1, lk1, lq2, lk2, subln, w_o, layer_idx):
    B, T, _ = h.shape
    q, k, v = jnp.split(h @ w_qkv, 3, axis=-1)
    q = q.reshape(B, T, ATTN_HEADS, 2, ATTN_HEAD_DIM)
    k = k.reshape(B, T, ATTN_HEADS, 2, ATTN_HEAD_DIM)
    v = v.reshape(B, T, ATTN_HEADS, ATTN_V_DIM)
    lam_init = 0.8 - 0.6 * math.exp(-0.3 * layer_idx)
    lam = (jnp.exp(jnp.sum(lq1.astype(jnp.float32) * lk1.astype(jnp.float32)))
           - jnp.exp(jnp.sum(lq2.astype(jnp.float32) * lk2.astype(jnp.float32))) + lam_init)
    if k_past is None:
        past = 0
        k_all, v_all = k, v
    else:
        past = k_past.shape[1]
        k_all = jnp.concatenate([k_past.astype(k.dtype), k], axis=1)
        v_all = jnp.concatenate([v_past.astype(v.dtype), v], axis=1)
    k_all = k_all.astype(jnp.float32)
    v_all = v_all.astype(jnp.float32)
    q32 = q.astype(jnp.float32)
    q_pos = past + jnp.arange(T, dtype=jnp.int32)
    k_pos = jnp.arange(past + T, dtype=jnp.int32)
    slopes = alibi_slopes()
    if T <= QUERY_BLOCK:
        o = diff_attention_core(q32, k_all, v_all, q_pos, k_pos, lam, slopes)
    else:
        nb = T // QUERY_BLOCK
        q_blocks = jnp.moveaxis(q32.reshape(B, nb, QUERY_BLOCK, ATTN_HEADS, 2, ATTN_HEAD_DIM), 1, 0)
        pos_blocks = q_pos.reshape(nb, QUERY_BLOCK)
        o = lax.map(lambda a: diff_attention_core(a[0], k_all, v_all, a[1], k_pos, lam, slopes),
                    (q_blocks, pos_blocks))
        o = jnp.moveaxis(o, 0, 1).reshape(B, T, ATTN_HEADS, ATTN_V_DIM)
    o = rms_norm(o, subln) * (1.0 - lam_init)
    out = o.reshape(B, T, ATTN_HEADS * ATTN_V_DIM).astype(h.dtype) @ w_o
    return out, k, v


def gated_linear_recurrence(q, k, v, log_f, s0, chunk):
    B, T, H, DK = q.shape
    DV = v.shape[-1]
    n = T // chunk

    def blocks(a):
        return jnp.moveaxis(a.reshape(B, n, chunk, *a.shape[2:]), 1, 0)

    causal = jnp.tril(jnp.ones((chunk, chunk), dtype=bool))[None, :, :, None, None]

    def step(state, inp):
        qc, kc, vc, gc = inp
        b = jnp.cumsum(gc, axis=1)
        diff = b[:, :, None] - b[:, None, :]
        decay = jnp.exp(jnp.where(causal, diff, -jnp.inf))
        scores = jnp.einsum('bthd,bshd,btshd->bhts', qc, kc, decay)
        o = (jnp.einsum('bhts,bshv->bthv', scores, vc)
             + jnp.einsum('bthd,bhdv->bthv', qc * jnp.exp(b), state))
        b_last = b[:, -1]
        new_state = (jnp.exp(b_last)[..., None] * state
                     + jnp.einsum('bshd,bshv->bhdv', kc * jnp.exp(b_last[:, None] - b), vc))
        return new_state, o

    s_fin, o = lax.scan(step, s0, (blocks(q), blocks(k), blocks(v), blocks(log_f)))
    o = jnp.moveaxis(o, 0, 1).reshape(B, T, H, DV)
    return o, s_fin


def hgrn2(h, s0, w_qfig, lb, out_norm, w_o):
    B, T, _ = h.shape
    q, fz, i, g = jnp.split(h @ w_qfig, 4, axis=-1)
    log_f = jnp.logaddexp(jnp.log(lb), jnp.log1p(-lb) + jax.nn.log_sigmoid(fz.astype(jnp.float32)))
    k = -jnp.expm1(log_f)
    q = jax.nn.silu(q.astype(jnp.float32))
    shp_k = (B, T, REC_HEADS, REC_DK)
    shp_v = (B, T, REC_HEADS, REC_DV)
    chunk = CHUNK if T % CHUNK == 0 else T
    o, s_fin = gated_linear_recurrence(q.reshape(shp_k), k.reshape(shp_k),
                                       i.astype(jnp.float32).reshape(shp_v),
                                       log_f.reshape(shp_k), s0.astype(jnp.float32), chunk)
    o = rms_norm(o, out_norm) * jax.nn.silu(g.astype(jnp.float32)).reshape(shp_v)
    out = o.reshape(B, T, REC_HEADS * REC_DV).astype(h.dtype) @ w_o
    return out, s_fin


def conv_ffn(h, prev, w_up, w_conv, b_conv, w_down):
    T = h.shape[1]
    gate, val = jnp.split(h @ w_up, 2, axis=-1)
    hp = jnp.concatenate([prev.astype(gate.dtype), gate], axis=1)
    c = b_conv
    for j in range(CONV_W):
        c = c + hp[:, j:j + T] * w_conv[j]
    out = (jax.nn.silu(c) * val) @ w_down
    return out, hp[:, -(CONV_W - 1):]


def setup_inputs(seed: int = 0) -> dict:
    key = jax.random.key(seed)
    ks = jax.random.split(key, 24)

    def nrm(k, shape, scale):
        return jax.random.normal(k, shape, jnp.float32) * scale

    qkv_out = 3 * ATTN_HEADS * 2 * ATTN_HEAD_DIM
    attn_in = ATTN_HEADS * ATTN_V_DIM
    return {
        "x_prompt": nrm(ks[0], (BATCH, SEQ, D_MODEL), 1.0),
        "x_sample": nrm(ks[1], (DEC_BATCH, DEC_SEQ, D_MODEL), 1.0),
        "cache_k": nrm(ks[2], (N_ATTN_LAYERS, DEC_BATCH, PAST_LEN, ATTN_HEADS, 2, ATTN_HEAD_DIM), 1.0),
        "cache_v": nrm(ks[3], (N_ATTN_LAYERS, DEC_BATCH, PAST_LEN, ATTN_HEADS, ATTN_V_DIM), 1.0),
        "state_hgrn": nrm(ks[4], (N_REC_LAYERS, DEC_BATCH, REC_HEADS, REC_DK, REC_DV), 0.3),
        "state_conv": nrm(ks[5], (DEPTH, DEC_BATCH, CONV_W - 1, D_FF), 1.0),
        "mixer_norm": 1.0 + nrm(ks[6], (DEPTH, D_MODEL), 0.02),
        "ffn_norm": 1.0 + nrm(ks[7], (DEPTH, D_MODEL), 0.02),
        "attn_w_qkv": nrm(ks[8], (N_ATTN_LAYERS, D_MODEL, qkv_out), D_MODEL ** -0.5),
        "attn_lambda_q1": nrm(ks[9], (N_ATTN_LAYERS, ATTN_HEAD_DIM), 0.1),
        "attn_lambda_k1": nrm(ks[10], (N_ATTN_LAYERS, ATTN_HEAD_DIM), 0.1),
        "attn_lambda_q2": nrm(ks[11], (N_ATTN_LAYERS, ATTN_HEAD_DIM), 0.1),
        "attn_lambda_k2": nrm(ks[12], (N_ATTN_LAYERS, ATTN_HEAD_DIM), 0.1),
        "attn_subln": 1.0 + nrm(ks[13], (N_ATTN_LAYERS, ATTN_V_DIM), 0.02),
        "attn_w_o": nrm(ks[14], (N_ATTN_LAYERS, attn_in, D_MODEL), attn_in ** -0.5),
        "rec_w_qfig": nrm(ks[15], (N_REC_LAYERS, D_MODEL, 4 * REC_WIDTH), D_MODEL ** -0.5),
        "rec_lower_bounds": nrm(ks[16], (DEPTH, REC_WIDTH), 0.5),
        "rec_out_norm": 1.0 + nrm(ks[17], (N_REC_LAYERS, REC_DV), 0.02),
        "rec_w_o": nrm(ks[18], (N_REC_LAYERS, REC_HEADS * REC_DV, D_MODEL), (REC_HEADS * REC_DV) ** -0.5),
        "ffn_w_up": nrm(ks[19], (DEPTH, D_MODEL, 2 * D_FF), D_MODEL ** -0.5),
        "ffn_conv_w": nrm(ks[20], (DEPTH, CONV_W, D_FF), CONV_W ** -0.5),
        "ffn_conv_b": nrm(ks[21], (DEPTH, D_FF), 0.02),
        "ffn_w_down": nrm(ks[22], (DEPTH, D_FF, D_MODEL), D_FF ** -0.5),
        "final_norm": 1.0 + nrm(ks[23], (D_MODEL,), 0.02),
    }


def reference(x_prompt, x_sample, cache_k, cache_v, state_hgrn, state_conv,
              mixer_norm, ffn_norm, attn_w_qkv, attn_lambda_q1, attn_lambda_k1,
              attn_lambda_q2, attn_lambda_k2, attn_subln, attn_w_o,
              rec_w_qfig, rec_lower_bounds, rec_out_norm, rec_w_o,
              ffn_w_up, ffn_conv_w, ffn_conv_b, ffn_w_down, final_norm):
    xp, xs = x_prompt, x_sample
    lbs = jax.nn.softmax(rec_lower_bounds.astype(jnp.float32), axis=0)
    lb_all = jnp.cumsum(lbs, axis=0) - lbs[0]

    kp_l, vp_l, ks_l, vs_l = [], [], [], []
    sp_l, ss_l, cp_l, cs_l = [], [], [], []
    for i in range(DEPTH):
        j = i // N_MIXERS
        hp = rms_norm(xp, mixer_norm[i])
        hs = rms_norm(xs, mixer_norm[i])
        if i % N_MIXERS == 0:
            w = (attn_w_qkv[j], attn_lambda_q1[j], attn_lambda_k1[j], attn_lambda_q2[j],
                 attn_lambda_k2[j], attn_subln[j], attn_w_o[j], i)
            out_p, k_p, v_p = diff_attention(hp, None, None, *w)
            out_s, k_s, v_s = diff_attention(hs, cache_k[j], cache_v[j], *w)
            kp_l.append(k_p); vp_l.append(v_p); ks_l.append(k_s); vs_l.append(v_s)
        else:
            w = (rec_w_qfig[j], lb_all[i], rec_out_norm[j], rec_w_o[j])
            s0_p = jnp.zeros((xp.shape[0], REC_HEADS, REC_DK, REC_DV), jnp.float32)
            out_p, st_p = hgrn2(hp, s0_p, *w)
            out_s, st_s = hgrn2(hs, state_hgrn[j], *w)
            sp_l.append(st_p); ss_l.append(st_s)
        xp = xp + out_p
        xs = xs + out_s
        wf = (ffn_w_up[i], ffn_conv_w[i], ffn_conv_b[i], ffn_w_down[i])
        prev_p = jnp.zeros((xp.shape[0], CONV_W - 1, D_FF), xp.dtype)
        f_p, c_p = conv_ffn(rms_norm(xp, ffn_norm[i]), prev_p, *wf)
        f_s, c_s = conv_ffn(rms_norm(xs, ffn_norm[i]), state_conv[i], *wf)
        cp_l.append(c_p); cs_l.append(c_s)
        xp = xp + f_p
        xs = xs + f_s

    y_prompt = rms_norm(xp, final_norm)
    y_sample = rms_norm(xs, final_norm)
    return (y_prompt, y_sample,
            jnp.stack(kp_l), jnp.stack(vp_l), jnp.stack(ks_l), jnp.stack(vs_l),
            jnp.stack(sp_l), jnp.stack(ss_l), jnp.stack(cp_l), jnp.stack(cs_l))
```

```python
import functools
import math

import jax
import jax.numpy as jnp
from jax import lax
from jax.experimental import pallas as pl
from jax.experimental.pallas import tpu as pltpu

F32 = jnp.float32
BF16 = jnp.bfloat16

CHUNK = 64
EPS = 1e-6
ATTN_HEADS = 8
ATTN_HEAD_DIM = 128
ATTN_V_DIM = 2 * ATTN_HEAD_DIM
ATTN_SCALE = ATTN_HEAD_DIM ** -0.5
REC_HEADS = 16
REC_DK = 128
REC_DV = 128
CONV_W = 3
NEG_BIG = -1e30
EXP_CLAMP = 80.0

V7X_VMEM_LIMIT = 56 * 1024 * 1024

_NT = (((1,), (1,)), ((), ()))
_TN = (((0,), (0,)), ((), ()))


def _params(*sem):
    return pltpu.CompilerParams(dimension_semantics=sem, vmem_limit_bytes=V7X_VMEM_LIMIT)


def _rms(x, g):
    ms = jnp.mean(x * x, axis=-1, keepdims=True)
    return (x * lax.rsqrt(ms + EPS)) * g


def _silu(x):
    return x / (1.0 + jnp.exp(-x))


def _norm_matmul_kernel(x_ref, g_ref, w_ref, *rest, n_out):
    outs, xn_ref = rest[:n_out], rest[n_out]

    @pl.when(pl.program_id(1) == 0)
    def _():
        xn_ref[...] = _rms(x_ref[...], g_ref[...]).astype(BF16)

    acc = jnp.dot(xn_ref[...], w_ref[...], preferred_element_type=F32)
    for o in outs:
        o[...] = acc.astype(o.dtype)


def norm_matmul(x, g, w, col0, ncols, out_dtypes, tm, tn):
    m, d = x.shape
    assert m % tm == 0 and ncols % tn == 0 and col0 % tn == 0
    joff = col0 // tn
    return pl.pallas_call(
        functools.partial(_norm_matmul_kernel, n_out=len(out_dtypes)),
        grid=(m // tm, ncols // tn),
        in_specs=[
            pl.BlockSpec((tm, d), lambda i, j: (i, 0)),
            pl.BlockSpec((1, d), lambda i, j: (0, 0)),
            pl.BlockSpec((d, tn), lambda i, j: (0, j + joff)),
        ],
        out_specs=[pl.BlockSpec((tm, tn), lambda i, j: (i, j)) for _ in out_dtypes],
        out_shape=[jax.ShapeDtypeStruct((m, ncols), dt) for dt in out_dtypes],
        scratch_shapes=[pltpu.VMEM((tm, d), BF16)],
        compiler_params=_params("parallel", "arbitrary"),
        name="norm_matmul",
    )(x, g.reshape(1, d), w)


def _matmul_res_kernel(a_ref, w_ref, r_ref, o_ref):
    o_ref[...] = r_ref[...] + jnp.dot(a_ref[...], w_ref[...], preferred_element_type=F32)


def matmul_res(a, w, res, tm, tn):
    m, k = a.shape
    n = w.shape[1]
    assert m % tm == 0 and n % tn == 0
    return pl.pallas_call(
        _matmul_res_kernel,
        grid=(m // tm, n // tn),
        in_specs=[
            pl.BlockSpec((tm, k), lambda i, j: (i, 0)),
            pl.BlockSpec((k, tn), lambda i, j: (0, j)),
            pl.BlockSpec((tm, tn), lambda i, j: (i, j)),
        ],
        out_specs=pl.BlockSpec((tm, tn), lambda i, j: (i, j)),
        out_shape=jax.ShapeDtypeStruct((m, n), F32),
        compiler_params=_params("parallel", "arbitrary"),
        name="matmul_res",
    )(a, w, res)


def _ffn_kernel(x_ref, prev_ref, g_ref, wg_ref, wv_ref, cw_ref, cb_ref, wd_ref, fg_ref,
                y_ref, cs_ref, xn_ref, ext_ref, carry_ref, *, final_norm):
    l = pl.program_id(1)
    j = pl.program_id(2)
    nj = pl.num_programs(2)
    bs, lt, d = x_ref.shape
    tf = wg_ref.shape[1]
    rows = bs * lt
    halo = CONV_W - 1

    @pl.when(j == 0)
    def _():
        x = x_ref[...].reshape(rows, d)
        xn_ref[...] = _rms(x, g_ref[...]).astype(BF16)
        y_ref[...] = x_ref[...]

    xn = xn_ref[...]
    gate = jnp.dot(xn, wg_ref[...], preferred_element_type=F32)
    val = jnp.dot(xn, wv_ref[...], preferred_element_type=F32)
    gate3 = gate.reshape(bs, lt, tf)

    @pl.when(l == 0)
    def _():
        ext_ref[:, 8 - halo:8, :] = prev_ref[...]

    @pl.when(l > 0)
    def _():
        ext_ref[:, 8 - halo:8, :] = carry_ref[j]

    ext_ref[:, 8:8 + lt, :] = gate3
    last = gate3[:, lt - halo:lt, :]
    carry_ref[j] = last
    cs_ref[...] = last

    c = cb_ref[...].reshape(1, 1, tf)
    for t in range(CONV_W):
        off = 8 - halo + t
        c = c + ext_ref[:, off:off + lt, :] * cw_ref[t:t + 1, :].reshape(1, 1, tf)
    act = (_silu(c) * val.reshape(bs, lt, tf)).reshape(rows, tf).astype(BF16)
    upd = jnp.dot(act, wd_ref[...], preferred_element_type=F32)
    y_ref[...] += upd.reshape(bs, lt, d)

    if final_norm:
        @pl.when(j == nj - 1)
        def _():
            y_ref[...] = _rms(y_ref[...], fg_ref[...].reshape(1, 1, d))


def conv_ffn(x3, prev, g, w_up, conv_w, conv_b, w_down, final_g, bs, lt, tf):
    nseq, L, d = x3.shape
    dff = w_down.shape[0]
    assert nseq % bs == 0 and L % lt == 0 and dff % tf == 0 and lt % 8 == 0
    nj = dff // tf
    halo = CONV_W - 1
    final_norm = final_g is not None
    fg = (final_g if final_norm else g).reshape(1, d)
    return pl.pallas_call(
        functools.partial(_ffn_kernel, final_norm=final_norm),
        grid=(nseq // bs, L // lt, nj),
        in_specs=[
            pl.BlockSpec((bs, lt, d), lambda s, l, j: (s, l, 0)),
            pl.BlockSpec((bs, halo, tf), lambda s, l, j: (s, 0, j)),
            pl.BlockSpec((1, d), lambda s, l, j: (0, 0)),
            pl.BlockSpec((d, tf), lambda s, l, j: (0, j)),
            pl.BlockSpec((d, tf), lambda s, l, j: (0, j + nj)),
            pl.BlockSpec((CONV_W, tf), lambda s, l, j: (0, j)),
            pl.BlockSpec((1, tf), lambda s, l, j: (0, j)),
            pl.BlockSpec((tf, d), lambda s, l, j: (j, 0)),
            pl.BlockSpec((1, d), lambda s, l, j: (0, 0)),
        ],
        out_specs=[
            pl.BlockSpec((bs, lt, d), lambda s, l, j: (s, l, 0)),
            pl.BlockSpec((bs, halo, tf), lambda s, l, j: (s, 0, j)),
        ],
        out_shape=[
            jax.ShapeDtypeStruct((nseq, L, d), F32),
            jax.ShapeDtypeStruct((nseq, halo, dff), F32),
        ],
        scratch_shapes=[
            pltpu.VMEM((bs * lt, d), BF16),
            pltpu.VMEM((bs, lt + 8, tf), F32),
            pltpu.VMEM((nj, bs, halo, tf), F32),
        ],
        compiler_params=_params("parallel", "arbitrary", "arbitrary"),
        name="conv_ffn",
    )(x3, prev, g.reshape(1, d), w_up, w_up, conv_w, conv_b.reshape(1, dff), w_down, fg)


def _softmax_step(s, m, l, acc, vb):
    m_new = jnp.maximum(m, jnp.max(s, axis=-1, keepdims=True))
    alpha = jnp.exp(m - m_new)
    p = jnp.exp(s - m_new)
    l_new = alpha * l + jnp.sum(p, axis=-1, keepdims=True)
    acc_new = alpha * acc + jnp.dot(p.astype(BF16), vb, preferred_element_type=F32)
    return m_new, l_new, acc_new


def _subln(o, g, lam_init):
    return _rms(o, g) * (1.0 - lam_init)


def _attn_prompt_kernel(slopes_ref, lam_ref, q_ref, k_ref, v_ref, g_ref, o_ref, *, tk, lam_init):
    h = pl.program_id(1)
    qi = pl.program_id(2)
    tq = q_ref.shape[0]
    dh = ATTN_HEAD_DIM
    slope = slopes_ref[h]
    lam = lam_ref[0]
    q = q_ref[...]
    r_io = lax.broadcasted_iota(jnp.int32, (tq, tk), 0)
    c_io = lax.broadcasted_iota(jnp.int32, (tq, tk), 1)
    rel = (r_io - c_io).astype(F32)
    q0 = qi * tq

    def step(kt, carry, masked):
        k0 = pl.multiple_of(kt * tk, tk)
        kb = k_ref[pl.ds(k0, tk), :]
        vb = v_ref[pl.ds(k0, tk), :]
        bias = slope * jnp.abs(rel + (q0 - k0).astype(F32))
        if masked:
            allowed = ((c_io + k0) // CHUNK) <= ((r_io + q0) // CHUNK)
        out = []
        for c in range(2):
            m, l, acc = carry[c]
            s = lax.dot_general(q[:, c * dh:(c + 1) * dh], kb[:, c * dh:(c + 1) * dh], _NT,
                                preferred_element_type=F32) * ATTN_SCALE - bias
            if masked:
                s = jnp.where(allowed, s, NEG_BIG)
            out.append(_softmax_step(s, m, l, acc, vb))
        return tuple(out)

    one = (jnp.full((tq, 1), NEG_BIG, F32), jnp.zeros((tq, 1), F32), jnp.zeros((tq, 2 * dh), F32))
    n_full = q0 // tk
    carry = lax.fori_loop(0, n_full, lambda kt, cr: step(kt, cr, False), (one, one))
    (_, l0, a0), (_, l1, a1) = step(n_full, carry, True)
    o = a0 / l0 - lam * (a1 / l1)
    o_ref[...] = _subln(o, g_ref[...], lam_init).astype(o_ref.dtype)


def attn_prompt(q, k, v, slopes, lam, subln, lam_init, tq, tk):
    b, t, _ = q.shape
    hw = 2 * ATTN_HEAD_DIM
    assert t % tk == 0 and tk % tq == 0 and tq % CHUNK == 0
    smem = pl.BlockSpec(memory_space=pltpu.SMEM)
    return pl.pallas_call(
        functools.partial(_attn_prompt_kernel, tk=tk, lam_init=lam_init),
        grid=(b, ATTN_HEADS, t // tq),
        in_specs=[
            smem, smem,
            pl.BlockSpec((None, tq, hw), lambda b, h, i: (b, i, h)),
            pl.BlockSpec((None, t, hw), lambda b, h, i: (b, 0, h)),
            pl.BlockSpec((None, t, hw), lambda b, h, i: (b, 0, h)),
            pl.BlockSpec((1, hw), lambda b, h, i: (0, 0)),
        ],
        out_specs=pl.BlockSpec((None, tq, hw), lambda b, h, i: (b, i, h)),
        out_shape=jax.ShapeDtypeStruct(q.shape, BF16),
        compiler_params=_params("parallel", "parallel", "arbitrary"),
        name="attn_prompt",
    )(slopes, lam, q, k, v, subln.reshape(1, hw))


def _attn_sample_kernel(slopes_ref, lam_ref, q_ref, kc_ref, vc_ref, kn_ref, vn_ref, g_ref, o_ref,
                        m_ref, l_ref, acc_ref, *, past, lam_init):
    kt = pl.program_id(1)
    nkt = pl.num_programs(1)
    nq = q_ref.shape[0]
    tk = kc_ref.shape[0]
    dh = ATTN_HEAD_DIM
    hw = 2 * dh
    lam = lam_ref[0]

    @pl.when(kt == 0)
    def _():
        m_ref[...] = jnp.full(m_ref.shape, NEG_BIG, F32)
        l_ref[...] = jnp.zeros(l_ref.shape, F32)
        acc_ref[...] = jnp.zeros(acc_ref.shape, F32)

    def scores(h, kb):
        qh = q_ref[:, h * hw:(h + 1) * hw]
        s0 = lax.dot_general(qh[:, :dh], kb[:, :dh], _NT, preferred_element_type=F32)
        s1 = lax.dot_general(qh[:, dh:], kb[:, dh:], _NT, preferred_element_type=F32)
        return jnp.concatenate([s0, s1], axis=0) * ATTN_SCALE

    def update(h, s, vb):
        m, l, acc = _softmax_step(s, m_ref[h][:, :1], l_ref[h][:, :1], acc_ref[h], vb)
        m_ref[h] = jnp.broadcast_to(m, m_ref.shape[1:])
        l_ref[h] = jnp.broadcast_to(l, l_ref.shape[1:])
        acc_ref[h] = acc

    r_io = lax.broadcasted_iota(jnp.int32, (2 * nq, tk), 0)
    c_io = lax.broadcasted_iota(jnp.int32, (2 * nq, tk), 1)
    dist = (past + r_io % nq - (kt * tk + c_io)).astype(F32)
    for h in range(ATTN_HEADS):
        kb = kc_ref[:, h * hw:(h + 1) * hw].astype(BF16)
        vb = vc_ref[:, h * hw:(h + 1) * hw].astype(BF16)
        update(h, scores(h, kb) - slopes_ref[h] * dist, vb)

    @pl.when(kt == nkt - 1)
    def _():
        rn = lax.broadcasted_iota(jnp.int32, (2 * nq, nq), 0) % nq + past
        cn = lax.broadcasted_iota(jnp.int32, (2 * nq, nq), 1) + past
        dist_n = jnp.abs(rn - cn).astype(F32)
        allowed = (cn // CHUNK) <= (rn // CHUNK)
        for h in range(ATTN_HEADS):
            kb = kn_ref[:, h * hw:(h + 1) * hw]
            vb = vn_ref[:, h * hw:(h + 1) * hw]
            s = jnp.where(allowed, scores(h, kb) - slopes_ref[h] * dist_n, NEG_BIG)
            update(h, s, vb)
            on = acc_ref[h] / l_ref[h][:, :1]
            o = on[:nq] - lam * on[nq:]
            o_ref[:, h * hw:(h + 1) * hw] = _subln(o, g_ref[...], lam_init).astype(o_ref.dtype)


def attn_sample(q, k_new, v_new, k_cache, v_cache, slopes, lam, subln, lam_init, tk):
    b, nq, d = q.shape
    past = k_cache.shape[1]
    hw = 2 * ATTN_HEAD_DIM
    assert past % tk == 0 and past > 0
    smem = pl.BlockSpec(memory_space=pltpu.SMEM)
    return pl.pallas_call(
        functools.partial(_attn_sample_kernel, past=past, lam_init=lam_init),
        grid=(b, past // tk),
        in_specs=[
            smem, smem,
            pl.BlockSpec((None, nq, d), lambda b, k: (b, 0, 0)),
            pl.BlockSpec((None, tk, d), lambda b, k: (b, k, 0)),
            pl.BlockSpec((None, tk, d), lambda b, k: (b, k, 0)),
            pl.BlockSpec((None, nq, d), lambda b, k: (b, 0, 0)),
            pl.BlockSpec((None, nq, d), lambda b, k: (b, 0, 0)),
            pl.BlockSpec((1, hw), lambda b, k: (0, 0)),
        ],
        out_specs=pl.BlockSpec((None, nq, d), lambda b, k: (b, 0, 0)),
        out_shape=jax.ShapeDtypeStruct(q.shape, BF16),
        scratch_shapes=[
            pltpu.VMEM((ATTN_HEADS, 2 * nq, 128), F32),
            pltpu.VMEM((ATTN_HEADS, 2 * nq, 128), F32),
            pltpu.VMEM((ATTN_HEADS, 2 * nq, hw), F32),
        ],
        compiler_params=_params("parallel", "arbitrary"),
        name="attn_sample",
    )(slopes, lam, q, k_cache, v_cache, k_new, v_new, subln.reshape(1, hw))


def _hgrn_kernel(q_ref, z_ref, i_ref, g_ref, lb_ref, gn_ref, s0_ref, o_ref, sout_ref,
                 s_ref, qb_ref, kb_ref, *, chunk, sub):
    tb = pl.program_id(2)
    tbs = q_ref.shape[0]
    dk = REC_DK
    nsub = chunk // sub

    @pl.when(tb == 0)
    def _():
        s_ref[...] = s0_ref[...]
        qb_ref[...] = jnp.zeros(qb_ref.shape, BF16)
        kb_ref[...] = jnp.zeros(kb_ref.shape, BF16)

    one_m_lb = 1.0 - lb_ref[...]
    r_io = lax.broadcasted_iota(jnp.int32, (chunk, chunk), 0)
    c_io = lax.broadcasted_iota(jnp.int32, (chunk, chunk), 1)
    causal = c_io <= r_io
    tri = causal.astype(F32)

    def body(ci, carry):
        t0 = pl.multiple_of(ci * chunk, chunk)
        z = z_ref[pl.ds(t0, chunk), :]
        kk = one_m_lb / (1.0 + jnp.exp(z))
        logf = jnp.log1p(-kk)
        b = jnp.dot(tri, logf, precision=lax.Precision.HIGHEST,
                    preferred_element_type=F32)
        qs = _silu(q_ref[pl.ds(t0, chunk), :])
        v = i_ref[pl.ds(t0, chunk), :].astype(BF16)
        b_last = b[chunk - 1:chunk, :]
        for i in range(nsub):
            lo, hi = sub * i, sub * (i + 1)
            ref = b[lo - 1:lo, :] if i > 0 else jnp.zeros((1, dk), F32)
            qb_ref[lo:hi, dk * i:dk * (i + 1)] = (qs[lo:hi] * jnp.exp(b[lo:hi] - ref)).astype(BF16)
            kb_ref[0:hi, dk * i:dk * (i + 1)] = (
                kk[:hi] * jnp.exp(jnp.minimum(ref - b[:hi], EXP_CLAMP))).astype(BF16)
        a = lax.dot_general(qb_ref[...], kb_ref[...], _NT, preferred_element_type=F32)
        a = jnp.where(causal, a, 0.0).astype(BF16)
        s_old = s_ref[...]
        o = (jnp.dot(a, v, preferred_element_type=F32)
             + jnp.dot((qs * jnp.exp(b)).astype(BF16), s_old.astype(BF16),
                       preferred_element_type=F32))
        k_end = (kk * jnp.exp(b_last - b)).astype(BF16)
        decay = jnp.transpose(jnp.broadcast_to(jnp.exp(b_last), (REC_DV, dk)))
        s_ref[...] = decay * s_old + lax.dot_general(k_end, v, _TN, preferred_element_type=F32)
        y = _rms(o, gn_ref[...]) * _silu(g_ref[pl.ds(t0, chunk), :])
        o_ref[pl.ds(t0, chunk), :] = y.astype(o_ref.dtype)
        return carry

    lax.fori_loop(0, tbs // chunk, body, 0)
    sout_ref[...] = s_ref[...]


def hgrn(qfig, s0, lb, out_norm, tbs, chunk, sub):
    b, t, _ = qfig.shape
    dk, dv, nh = REC_DK, REC_DV, REC_HEADS
    assert t % tbs == 0 and tbs % chunk == 0 and chunk % sub == 0 and sub % 16 == 0
    nsub = chunk // sub

    def col(kind):
        return pl.BlockSpec((None, tbs, dk), lambda b, h, i: (b, i, kind * nh + h))

    return pl.pallas_call(
        functools.partial(_hgrn_kernel, chunk=chunk, sub=sub),
        grid=(b, nh, t // tbs),
        in_specs=[
            col(0), col(1), col(2), col(3),
            pl.BlockSpec((1, dk), lambda b, h, i: (0, h)),
            pl.BlockSpec((1, dv), lambda b, h, i: (0, 0)),
            pl.BlockSpec((None, None, dk, dv), lambda b, h, i: (b, h, 0, 0)),
        ],
        out_specs=[
            pl.BlockSpec((None, tbs, dv), lambda b, h, i: (b, i, h)),
            pl.BlockSpec((None, None, dk, dv), lambda b, h, i: (b, h, 0, 0)),
        ],
        out_shape=[
            jax.ShapeDtypeStruct((b, t, nh * dv), BF16),
            jax.ShapeDtypeStruct((b, nh, dk, dv), F32),
        ],
        scratch_shapes=[
            pltpu.VMEM((dk, dv), F32),
            pltpu.VMEM((chunk, nsub * dk), BF16),
            pltpu.VMEM((chunk, nsub * dk), BF16),
        ],
        compiler_params=_params("parallel", "parallel", "arbitrary"),
        name="hgrn",
    )(qfig, qfig, qfig, qfig, lb.reshape(1, nh * dk), out_norm.reshape(1, dv), s0)


def _pick(n, candidates):
    for c in candidates:
        if n % c == 0:
            return c
    return n


def _trunk(x3, is_prompt, cache_k, cache_v, state_hgrn, state_conv, p):
    nseq, L, d = x3.shape
    m = nseq * L
    dff = p["ffn_w_down"].shape[1]
    tm = _pick(m, (1024, 512, 256))
    tf = _pick(dff, (512, 256, 128))
    if is_prompt:
        ffn_bs, ffn_lt = 1, _pick(L, (512, 256, 128))
    else:
        ffn_bs, ffn_lt = nseq, L
    x = x3.reshape(m, d)

    wqkv = p["attn_w_qkv"][0]
    nqk = ATTN_HEADS * 2 * ATTN_HEAD_DIM
    g0 = p["mixer_norm"][0]
    (q,) = norm_matmul(x, g0, wqkv, 0, nqk, (BF16,), tm, 512)
    k32, k16 = norm_matmul(x, g0, wqkv, nqk, nqk, (F32, BF16), tm, 512)
    v32, v16 = norm_matmul(x, g0, wqkv, 2 * nqk, nqk, (F32, BF16), tm, 512)
    sh = (nseq, L, nqk)
    lam_init = 0.8 - 0.6 * math.exp(-0.3 * 0)
    if is_prompt:
        o = attn_prompt(q.reshape(sh), k16.reshape(sh), v16.reshape(sh), p["slopes"], p["lam"],
                        p["attn_subln"][0], lam_init, tq=_pick(L, (256, 128, 64)),
                        tk=_pick(L, (512, 256, 128, 64)))
    else:
        past = cache_k.shape[2]
        o = attn_sample(q.reshape(sh), k16.reshape(sh), v16.reshape(sh),
                        cache_k[0].reshape(nseq, past, nqk), cache_v[0].reshape(nseq, past, nqk),
                        p["slopes"], p["lam"], p["attn_subln"][0], lam_init,
                        tk=_pick(past, (512, 256, 128)))
    x = matmul_res(o.reshape(m, nqk), p["attn_w_o"][0], x, tm, 512)
    prev0 = jnp.zeros((nseq, CONV_W - 1, dff), F32) if is_prompt else state_conv[0]
    x3b, cs0 = conv_ffn(x.reshape(nseq, L, d), prev0, p["ffn_norm"][0], p["ffn_w_up"][0],
                        p["ffn_conv_w"][0], p["ffn_conv_b"][0], p["ffn_w_down"][0], None,
                        ffn_bs, ffn_lt, tf)
    x = x3b.reshape(m, d)

    wq = p["rec_w_qfig"][0]
    (qfig,) = norm_matmul(x, p["mixer_norm"][1], wq, 0, wq.shape[1], (F32,), tm, 512)
    if is_prompt:
        s0 = jnp.zeros((nseq, REC_HEADS, REC_DK, REC_DV), F32)
        chunk = _pick(L, (128, 64, 32, 16))
        tbs, sub = _pick(L, (512, 256, 128, 64, 32, 16)), 16
    else:
        s0 = state_hgrn[0]
        chunk, tbs, sub = L, L, 16
    o, s_fin = hgrn(qfig.reshape(nseq, L, -1), s0, p["lb"], p["rec_out_norm"][0], tbs, chunk, sub)
    x = matmul_res(o.reshape(m, d), p["rec_w_o"][0], x, tm, 512)
    prev1 = jnp.zeros((nseq, CONV_W - 1, dff), F32) if is_prompt else state_conv[1]
    y, cs1 = conv_ffn(x.reshape(nseq, L, d), prev1, p["ffn_norm"][1], p["ffn_w_up"][1],
                      p["ffn_conv_w"][1], p["ffn_conv_b"][1], p["ffn_w_down"][1], p["final_norm"],
                      ffn_bs, ffn_lt, tf)
    kshape = (1, nseq, L, ATTN_HEADS, 2, ATTN_HEAD_DIM)
    vshape = (1, nseq, L, ATTN_HEADS, ATTN_V_DIM)
    return (y, k32.reshape(kshape), v32.reshape(vshape), s_fin[None], jnp.stack([cs0, cs1]))


def kernel(x_prompt, x_sample, cache_k, cache_v, state_hgrn, state_conv, mixer_norm, ffn_norm,
           attn_w_qkv, attn_lambda_q1, attn_lambda_k1, attn_lambda_q2, attn_lambda_k2, attn_subln,
           attn_w_o, rec_w_qfig, rec_lower_bounds, rec_out_norm, rec_w_o, ffn_w_up, ffn_conv_w,
           ffn_conv_b, ffn_w_down, final_norm):
    lam_init = 0.8 - 0.6 * math.exp(-0.3 * 0)
    lam = (jnp.exp(jnp.sum(attn_lambda_q1[0] * attn_lambda_k1[0]))
           - jnp.exp(jnp.sum(attn_lambda_q2[0] * attn_lambda_k2[0])) + lam_init).reshape(1)
    slopes = jnp.exp2(-8.0 * jnp.arange(1, ATTN_HEADS + 1, dtype=F32) / ATTN_HEADS)
    lbs = jax.nn.softmax(rec_lower_bounds.astype(F32), axis=0)
    lb = (jnp.cumsum(lbs, axis=0) - lbs[0])[1]
    p = dict(
        mixer_norm=mixer_norm, ffn_norm=ffn_norm, attn_subln=attn_subln, rec_out_norm=rec_out_norm,
        ffn_conv_w=ffn_conv_w, ffn_conv_b=ffn_conv_b, final_norm=final_norm,
        lam=lam, slopes=slopes, lb=lb,
        attn_w_qkv=attn_w_qkv.astype(BF16), attn_w_o=attn_w_o.astype(BF16),
        rec_w_qfig=rec_w_qfig.astype(BF16), rec_w_o=rec_w_o.astype(BF16),
        ffn_w_up=ffn_w_up.astype(BF16), ffn_w_down=ffn_w_down.astype(BF16),
    )
    yp, kp, vp, sp, cp = _trunk(x_prompt, True, None, None, None, None, p)
    ys, ks, vs, ss, cs = _trunk(x_sample, False, cache_k, cache_v, state_hgrn, state_conv, p)
    return (yp, ys, kp, vp, ks, vs, sp, ss, cp, cs)
```

```python
import functools
import math

import jax
import jax.numpy as jnp
from jax import lax
from jax.experimental import pallas as pl
from jax.experimental.pallas import tpu as pltpu

F32 = jnp.float32
BF16 = jnp.bfloat16

CHUNK = 64
EPS = 1e-6
ATTN_HEADS = 8
ATTN_HEAD_DIM = 128
ATTN_V_DIM = 2 * ATTN_HEAD_DIM
ATTN_SCALE = ATTN_HEAD_DIM ** -0.5
REC_HEADS = 16
REC_DK = 128
REC_DV = 128
CONV_W = 3
NEG_BIG = -1e30
EXP_CLAMP = 80.0

V7X_VMEM_LIMIT = 56 * 1024 * 1024

_NT = (((1,), (1,)), ((), ()))
_TN = (((0,), (0,)), ((), ()))


def _params(*sem):
    return pltpu.CompilerParams(dimension_semantics=sem, vmem_limit_bytes=V7X_VMEM_LIMIT)


def _rms(x, g):
    ms = jnp.mean(x * x, axis=-1, keepdims=True)
    return (x * lax.rsqrt(ms + EPS)) * g


def _silu(x):
    return x / (1.0 + jnp.exp(-x))


LANES = 128


def _norm_matmul_kernel(x_ref, g_ref, w_ref, *rest, layouts):
    outs, xn_ref = rest[:len(layouts)], rest[len(layouts)]

    @pl.when(pl.program_id(1) == 0)
    def _():
        xn_ref[...] = _rms(x_ref[...], g_ref[...]).astype(BF16)

    acc = jnp.dot(xn_ref[...], w_ref[...], preferred_element_type=F32)
    tm, tn = acc.shape
    ngrp = tn // LANES
    for o, layout in zip(outs, layouts):
        if layout == "rows":
            o[...] = acc.astype(o.dtype)
        elif layout == "groups":
            for gi in range(ngrp):
                o[gi] = acc[:, gi * LANES:(gi + 1) * LANES].astype(o.dtype)
        else:
            for gi in range(ngrp):
                o[pl.ds(gi, tm, stride=ngrp), :] = acc[:, gi * LANES:(gi + 1) * LANES].astype(o.dtype)


def norm_matmul(x, g, w, col0, ncols, outs, tm, tn):
    m, d = x.shape
    assert m % tm == 0 and ncols % tn == 0 and col0 % tn == 0 and tn % LANES == 0
    joff = col0 // tn
    ngrp = tn // LANES
    specs, shapes = [], []
    for dt, layout in outs:
        if layout == "rows":
            specs.append(pl.BlockSpec((tm, tn), lambda i, j: (i, j)))
            shapes.append(jax.ShapeDtypeStruct((m, ncols), dt))
        elif layout == "groups":
            specs.append(pl.BlockSpec((ngrp, tm, LANES), lambda i, j: (j, i, 0)))
            shapes.append(jax.ShapeDtypeStruct((ncols // LANES, m, LANES), dt))
        else:
            assert layout == "interleaved" and tn == ncols and dt == F32
            specs.append(pl.BlockSpec((tm * ngrp, LANES), lambda i, j: (i, 0)))
            shapes.append(jax.ShapeDtypeStruct((m * ngrp, LANES), dt))
    return pl.pallas_call(
        functools.partial(_norm_matmul_kernel, layouts=tuple(l for _, l in outs)),
        grid=(m // tm, ncols // tn),
        in_specs=[
            pl.BlockSpec((tm, d), lambda i, j: (i, 0)),
            pl.BlockSpec((1, d), lambda i, j: (0, 0)),
            pl.BlockSpec((d, tn), lambda i, j: (0, j + joff)),
        ],
        out_specs=specs,
        out_shape=shapes,
        scratch_shapes=[pltpu.VMEM((tm, d), BF16)],
        compiler_params=_params("parallel", "arbitrary"),
        name="norm_matmul",
    )(x, g.reshape(1, d), w)


def _matmul_res_kernel(a_ref, w_ref, r_ref, o_ref):
    o_ref[...] = r_ref[...] + jnp.dot(a_ref[...], w_ref[...], preferred_element_type=F32)


def matmul_res(a, w, res, tm, tn):
    m, k = a.shape
    n = w.shape[1]
    assert m % tm == 0 and n % tn == 0
    return pl.pallas_call(
        _matmul_res_kernel,
        grid=(m // tm, n // tn),
        in_specs=[
            pl.BlockSpec((tm, k), lambda i, j: (i, 0)),
            pl.BlockSpec((k, tn), lambda i, j: (0, j)),
            pl.BlockSpec((tm, tn), lambda i, j: (i, j)),
        ],
        out_specs=pl.BlockSpec((tm, tn), lambda i, j: (i, j)),
        out_shape=jax.ShapeDtypeStruct((m, n), F32),
        compiler_params=_params("parallel", "arbitrary"),
        name="matmul_res",
    )(a, w, res)


def _ffn_kernel(x_ref, prev_ref, g_ref, wg_ref, wv_ref, cw_ref, cb_ref, wd_ref, fg_ref,
                y_ref, cs_ref, xn_ref, ext_ref, carry_ref, *, final_norm):
    l = pl.program_id(1)
    j = pl.program_id(2)
    nj = pl.num_programs(2)
    bs, lt, d = x_ref.shape
    tf = wg_ref.shape[1]
    rows = bs * lt
    halo = CONV_W - 1

    @pl.when(j == 0)
    def _():
        x = x_ref[...].reshape(rows, d)
        xn_ref[...] = _rms(x, g_ref[...]).astype(BF16)
        y_ref[...] = x_ref[...]

    xn = xn_ref[...]
    gate = jnp.dot(xn, wg_ref[...], preferred_element_type=F32)
    val = jnp.dot(xn, wv_ref[...], preferred_element_type=F32)
    gate3 = gate.reshape(bs, lt, tf)

    @pl.when(l == 0)
    def _():
        ext_ref[:, 8 - halo:8, :] = prev_ref[...]

    @pl.when(l > 0)
    def _():
        ext_ref[:, 8 - halo:8, :] = carry_ref[j]

    ext_ref[:, 8:8 + lt, :] = gate3
    last = gate3[:, lt - halo:lt, :]
    carry_ref[j] = last
    cs_ref[...] = last

    c = cb_ref[...].reshape(1, 1, tf)
    for t in range(CONV_W):
        off = 8 - halo + t
        c = c + ext_ref[:, off:off + lt, :] * cw_ref[t:t + 1, :].reshape(1, 1, tf)
    act = (_silu(c) * val.reshape(bs, lt, tf)).reshape(rows, tf).astype(BF16)
    upd = jnp.dot(act, wd_ref[...], preferred_element_type=F32)
    y_ref[...] += upd.reshape(bs, lt, d)

    if final_norm:
        @pl.when(j == nj - 1)
        def _():
            y_ref[...] = _rms(y_ref[...], fg_ref[...].reshape(1, 1, d))


def conv_ffn(x3, prev, g, w_up, conv_w, conv_b, w_down, final_g, bs, lt, tf):
    nseq, L, d = x3.shape
    dff = w_down.shape[0]
    assert nseq % bs == 0 and L % lt == 0 and dff % tf == 0 and lt % 8 == 0
    nj = dff // tf
    halo = CONV_W - 1
    final_norm = final_g is not None
    fg = (final_g if final_norm else g).reshape(1, d)
    y, tails = pl.pallas_call(
        functools.partial(_ffn_kernel, final_norm=final_norm),
        grid=(nseq // bs, L // lt, nj),
        in_specs=[
            pl.BlockSpec((bs, lt, d), lambda s, l, j: (s, l, 0)),
            pl.BlockSpec((bs, halo, tf), lambda s, l, j: (s, 0, j)),
            pl.BlockSpec((1, d), lambda s, l, j: (0, 0)),
            pl.BlockSpec((d, tf), lambda s, l, j: (0, j)),
            pl.BlockSpec((d, tf), lambda s, l, j: (0, j + nj)),
            pl.BlockSpec((CONV_W, tf), lambda s, l, j: (0, j)),
            pl.BlockSpec((1, tf), lambda s, l, j: (0, j)),
            pl.BlockSpec((tf, d), lambda s, l, j: (j, 0)),
            pl.BlockSpec((1, d), lambda s, l, j: (0, 0)),
        ],
        out_specs=[
            pl.BlockSpec((bs, lt, d), lambda s, l, j: (s, l, 0)),
            pl.BlockSpec((bs, None, halo, tf), lambda s, l, j: (s, l, 0, j)),
        ],
        out_shape=[
            jax.ShapeDtypeStruct((nseq, L, d), F32),
            jax.ShapeDtypeStruct((nseq, L // lt, halo, dff), F32),
        ],
        scratch_shapes=[
            pltpu.VMEM((bs * lt, d), BF16),
            pltpu.VMEM((bs, lt + 8, tf), F32),
            pltpu.VMEM((nj, bs, halo, tf), F32),
        ],
        compiler_params=_params("parallel", "arbitrary", "arbitrary"),
        name="conv_ffn",
    )(x3, prev, g.reshape(1, d), w_up, w_up, conv_w, conv_b.reshape(1, dff), w_down, fg)
    return y, tails[:, -1]


def _softmax_step(s, m, l, acc, vb):
    m_new = jnp.maximum(m, jnp.max(s, axis=-1, keepdims=True))
    alpha = jnp.exp(m - m_new)
    p = jnp.exp(s - m_new)
    l_new = alpha * l + jnp.sum(p, axis=-1, keepdims=True)
    acc_new = alpha * acc + jnp.dot(p.astype(BF16), vb, preferred_element_type=F32)
    return m_new, l_new, acc_new


def _subln(o, g, lam_init):
    return _rms(o, g) * (1.0 - lam_init)


def _attn_prompt_kernel(slopes_ref, lam_ref, q_ref, k_ref, v_ref, g_ref, o_ref, *, tk, lam_init):
    h = pl.program_id(1)
    qi = pl.program_id(2)
    tq = q_ref.shape[0]
    dh = ATTN_HEAD_DIM
    slope = slopes_ref[h]
    lam = lam_ref[0]
    q = q_ref[...]
    r_io = lax.broadcasted_iota(jnp.int32, (tq, tk), 0)
    c_io = lax.broadcasted_iota(jnp.int32, (tq, tk), 1)
    rel = (r_io - c_io).astype(F32)
    q0 = qi * tq

    def step(kt, carry, masked):
        k0 = pl.multiple_of(kt * tk, tk)
        kb = k_ref[pl.ds(k0, tk), :]
        vb = v_ref[pl.ds(k0, tk), :]
        bias = slope * jnp.abs(rel + (q0 - k0).astype(F32))
        if masked:
            allowed = ((c_io + k0) // CHUNK) <= ((r_io + q0) // CHUNK)
        out = []
        for c in range(2):
            m, l, acc = carry[c]
            s = lax.dot_general(q[:, c * dh:(c + 1) * dh], kb[:, c * dh:(c + 1) * dh], _NT,
                                preferred_element_type=F32) * ATTN_SCALE - bias
            if masked:
                s = jnp.where(allowed, s, NEG_BIG)
            out.append(_softmax_step(s, m, l, acc, vb))
        return tuple(out)

    one = (jnp.full((tq, 1), NEG_BIG, F32), jnp.zeros((tq, 1), F32), jnp.zeros((tq, 2 * dh), F32))
    n_full = q0 // tk
    carry = lax.fori_loop(0, n_full, lambda kt, cr: step(kt, cr, False), (one, one))
    (_, l0, a0), (_, l1, a1) = step(n_full, carry, True)
    o = a0 / l0 - lam * (a1 / l1)
    o_ref[...] = _subln(o, g_ref[...], lam_init).astype(o_ref.dtype)


def attn_prompt(q, k, v, slopes, lam, subln, lam_init, tq, tk):
    b, t, _ = q.shape
    hw = 2 * ATTN_HEAD_DIM
    assert t % tk == 0 and tk % tq == 0 and tq % CHUNK == 0
    smem = pl.BlockSpec(memory_space=pltpu.SMEM)
    return pl.pallas_call(
        functools.partial(_attn_prompt_kernel, tk=tk, lam_init=lam_init),
        grid=(b, ATTN_HEADS, t // tq),
        in_specs=[
            smem, smem,
            pl.BlockSpec((None, tq, hw), lambda b, h, i: (b, i, h)),
            pl.BlockSpec((None, t, hw), lambda b, h, i: (b, 0, h)),
            pl.BlockSpec((None, t, hw), lambda b, h, i: (b, 0, h)),
            pl.BlockSpec((1, hw), lambda b, h, i: (0, 0)),
        ],
        out_specs=pl.BlockSpec((None, tq, hw), lambda b, h, i: (b, i, h)),
        out_shape=jax.ShapeDtypeStruct(q.shape, BF16),
        compiler_params=_params("parallel", "parallel", "arbitrary"),
        name="attn_prompt",
    )(slopes, lam, q, k, v, subln.reshape(1, hw))


def _attn_sample_kernel(slopes_ref, lam_ref, q_ref, kc_ref, vlo_ref, vhi_ref, kn_ref, vn_ref, g_ref,
                        o_ref, m_ref, l_ref, acc_ref, *, past, lam_init):
    kt = pl.program_id(1)
    nkt = pl.num_programs(1)
    nq = q_ref.shape[0]
    tk = vlo_ref.shape[0] // ATTN_HEADS
    dh = ATTN_HEAD_DIM
    hw = 2 * dh
    lam = lam_ref[0]

    @pl.when(kt == 0)
    def _():
        m_ref[...] = jnp.full(m_ref.shape, NEG_BIG, F32)
        l_ref[...] = jnp.zeros(l_ref.shape, F32)
        acc_ref[...] = jnp.zeros(acc_ref.shape, F32)

    def scores(h, kb):
        qh = q_ref[:, h * hw:(h + 1) * hw]
        s0 = lax.dot_general(qh[:, :dh], kb[:, :dh], _NT, preferred_element_type=F32)
        s1 = lax.dot_general(qh[:, dh:], kb[:, dh:], _NT, preferred_element_type=F32)
        return jnp.concatenate([s0, s1], axis=0) * ATTN_SCALE

    def update(h, s, vb):
        m, l, acc = _softmax_step(s, m_ref[h][:, :1], l_ref[h][:, :1], acc_ref[h], vb)
        m_ref[h] = jnp.broadcast_to(m, m_ref.shape[1:])
        l_ref[h] = jnp.broadcast_to(l, l_ref.shape[1:])
        acc_ref[h] = acc

    r_io = lax.broadcasted_iota(jnp.int32, (2 * nq, tk), 0)
    c_io = lax.broadcasted_iota(jnp.int32, (2 * nq, tk), 1)
    dist = (past + r_io % nq - (kt * tk + c_io)).astype(F32)
    for h in range(ATTN_HEADS):
        kb = jnp.concatenate(
            [kc_ref[pl.ds(2 * h + c, tk, stride=2 * ATTN_HEADS), :] for c in range(2)],
            axis=1).astype(BF16)
        vb = jnp.concatenate(
            [vlo_ref[pl.ds(h, tk, stride=ATTN_HEADS), :], vhi_ref[pl.ds(h, tk, stride=ATTN_HEADS), :]],
            axis=1).astype(BF16)
        update(h, scores(h, kb) - slopes_ref[h] * dist, vb)

    @pl.when(kt == nkt - 1)
    def _():
        rn = lax.broadcasted_iota(jnp.int32, (2 * nq, nq), 0) % nq + past
        cn = lax.broadcasted_iota(jnp.int32, (2 * nq, nq), 1) + past
        dist_n = jnp.abs(rn - cn).astype(F32)
        allowed = (cn // CHUNK) <= (rn // CHUNK)
        for h in range(ATTN_HEADS):
            kb = kn_ref[:, h * hw:(h + 1) * hw]
            vb = vn_ref[:, h * hw:(h + 1) * hw]
            s = jnp.where(allowed, scores(h, kb) - slopes_ref[h] * dist_n, NEG_BIG)
            update(h, s, vb)
            on = acc_ref[h] / l_ref[h][:, :1]
            o = on[:nq] - lam * on[nq:]
            o_ref[:, h * hw:(h + 1) * hw] = _subln(o, g_ref[...], lam_init).astype(o_ref.dtype)


def attn_sample(q, k_new, v_new, k_cache, v_cache, slopes, lam, subln, lam_init, tk):
    b, nq, d = q.shape
    nh, dh = ATTN_HEADS, ATTN_HEAD_DIM
    past = v_cache.shape[1] // nh
    hw = 2 * dh
    assert past % tk == 0 and past > 0 and dh == LANES
    smem = pl.BlockSpec(memory_space=pltpu.SMEM)
    return pl.pallas_call(
        functools.partial(_attn_sample_kernel, past=past, lam_init=lam_init),
        grid=(b, past // tk),
        in_specs=[
            smem, smem,
            pl.BlockSpec((None, nq, d), lambda b, k: (b, 0, 0)),
            pl.BlockSpec((None, tk * nh * 2, dh), lambda b, k: (b, k, 0)),
            pl.BlockSpec((None, tk * nh, dh), lambda b, k: (b, k, 0)),
            pl.BlockSpec((None, tk * nh, dh), lambda b, k: (b, k, 1)),
            pl.BlockSpec((None, nq, d), lambda b, k: (b, 0, 0)),
            pl.BlockSpec((None, nq, d), lambda b, k: (b, 0, 0)),
            pl.BlockSpec((1, hw), lambda b, k: (0, 0)),
        ],
        out_specs=pl.BlockSpec((None, nq, d), lambda b, k: (b, 0, 0)),
        out_shape=jax.ShapeDtypeStruct(q.shape, BF16),
        scratch_shapes=[
            pltpu.VMEM((ATTN_HEADS, 2 * nq, 128), F32),
            pltpu.VMEM((ATTN_HEADS, 2 * nq, 128), F32),
            pltpu.VMEM((ATTN_HEADS, 2 * nq, hw), F32),
        ],
        compiler_params=_params("parallel", "arbitrary"),
        name="attn_sample",
    )(slopes, lam, q, k_cache, v_cache, v_cache, k_new, v_new, subln.reshape(1, hw))


def _cumsum_rows(tri, x):
    hi = x.astype(BF16)
    r1 = x - hi.astype(F32)
    mid = r1.astype(BF16)
    lo = (r1 - mid.astype(F32)).astype(BF16)
    n = x.shape[1]
    y = jnp.dot(tri, jnp.concatenate([hi, mid, lo], axis=1), preferred_element_type=F32)
    return y[:, :n] + y[:, n:2 * n] + y[:, 2 * n:]


def _hgrn_kernel(q_ref, z_ref, i_ref, g_ref, lb_ref, gn_ref, s0_ref, o_ref, sout_ref,
                 s_ref, qb_ref, kb_ref, *, chunk, sub):
    tb = pl.program_id(2)
    hb, tbs, dk = q_ref.shape
    nch = tbs // chunk
    nsub = chunk // sub

    @pl.when(tb == 0)
    def _():
        s_ref[...] = s0_ref[...]
        qb_ref[...] = jnp.zeros(qb_ref.shape, BF16)
        kb_ref[...] = jnp.zeros(kb_ref.shape, BF16)

    r_io = lax.broadcasted_iota(jnp.int32, (chunk, chunk), 0)
    c_io = lax.broadcasted_iota(jnp.int32, (chunk, chunk), 1)
    causal = c_io <= r_io
    tri = causal.astype(BF16)

    for h in range(hb):
        one_m_lb = 1.0 - lb_ref[h]
        parts = []
        for c in range(nch):
            t0, slot = c * chunk, h * nch + c
            z = z_ref[h, t0:t0 + chunk, :]
            kk = one_m_lb / (1.0 + jnp.exp(z))
            b = _cumsum_rows(tri, jnp.log1p(-kk))
            qs = _silu(q_ref[h, t0:t0 + chunk, :])
            v = i_ref[h, t0:t0 + chunk, :].astype(BF16)
            b_last = b[chunk - 1:chunk, :]
            for i in range(nsub):
                lo, hi = sub * i, sub * (i + 1)
                ref = b[lo - 1:lo, :] if i > 0 else jnp.zeros((1, dk), F32)
                qb_ref[slot, lo:hi, dk * i:dk * (i + 1)] = (
                    qs[lo:hi] * jnp.exp(b[lo:hi] - ref)).astype(BF16)
                kb_ref[slot, 0:hi, dk * i:dk * (i + 1)] = (
                    kk[:hi] * jnp.exp(jnp.minimum(ref - b[:hi], EXP_CLAMP))).astype(BF16)
            a = lax.dot_general(qb_ref[slot], kb_ref[slot], _NT, preferred_element_type=F32)
            a = jnp.where(causal, a, 0.0).astype(BF16)
            o_intra = jnp.dot(a, v, preferred_element_type=F32)
            q_end = (qs * jnp.exp(b)).astype(BF16)
            k_end = (kk * jnp.exp(b_last - b)).astype(BF16)
            upd = lax.dot_general(k_end, v, _TN, preferred_element_type=F32)
            decay = jnp.transpose(jnp.broadcast_to(jnp.exp(b_last), (REC_DV, dk)))
            parts.append((o_intra, q_end, decay, upd))
        state = s_ref[h]
        for c in range(nch):
            t0 = c * chunk
            o_intra, q_end, decay, upd = parts[c]
            o = o_intra + jnp.dot(q_end, state.astype(BF16), preferred_element_type=F32)
            state = decay * state + upd
            y = _rms(o, gn_ref[...]) * _silu(g_ref[h, t0:t0 + chunk, :])
            o_ref[t0:t0 + chunk, h * REC_DV:(h + 1) * REC_DV] = y.astype(o_ref.dtype)
        s_ref[h] = state
    sout_ref[...] = s_ref[...]


def hgrn(qfig, s0, lb, out_norm, hb, tbs, chunk, sub):
    _, nh, b, t, dk = qfig.shape
    dv = REC_DV
    assert t % tbs == 0 and tbs % chunk == 0 and chunk % sub == 0 and sub % 16 == 0 and nh % hb == 0
    nsub = chunk // sub
    nslot = hb * (tbs // chunk)

    def kind(k):
        return pl.BlockSpec((None, hb, None, tbs, dk), lambda b, h, i: (k, h, b, i, 0))

    return pl.pallas_call(
        functools.partial(_hgrn_kernel, chunk=chunk, sub=sub),
        grid=(b, nh // hb, t // tbs),
        in_specs=[
            kind(0), kind(1), kind(2), kind(3),
            pl.BlockSpec((hb, 1, dk), lambda b, h, i: (h, 0, 0)),
            pl.BlockSpec((1, dv), lambda b, h, i: (0, 0)),
            pl.BlockSpec((None, hb, dk, dv), lambda b, h, i: (b, h, 0, 0)),
        ],
        out_specs=[
            pl.BlockSpec((None, tbs, hb * dv), lambda b, h, i: (b, i, h)),
            pl.BlockSpec((None, hb, dk, dv), lambda b, h, i: (b, h, 0, 0)),
        ],
        out_shape=[
            jax.ShapeDtypeStruct((b, t, nh * dv), BF16),
            jax.ShapeDtypeStruct((b, nh, dk, dv), F32),
        ],
        scratch_shapes=[
            pltpu.VMEM((hb, dk, dv), F32),
            pltpu.VMEM((nslot, chunk, nsub * dk), BF16),
            pltpu.VMEM((nslot, chunk, nsub * dk), BF16),
        ],
        compiler_params=_params("parallel", "parallel", "arbitrary"),
        name="hgrn",
    )(qfig, qfig, qfig, qfig, lb.reshape(nh, 1, dk), out_norm.reshape(1, dv), s0)


def _pick(n, candidates):
    for c in candidates:
        if n % c == 0:
            return c
    return n


def _trunk(x3, is_prompt, cache_k, cache_v, state_hgrn, state_conv, p):
    nseq, L, d = x3.shape
    m = nseq * L
    dff = p["ffn_w_down"].shape[1]
    tm = _pick(m, (1024, 512, 256))
    tf = _pick(dff, (512, 256, 128))
    if is_prompt:
        ffn_bs, ffn_lt = 1, _pick(L, (512, 256, 128))
    else:
        ffn_bs, ffn_lt = nseq, L
    x = x3.reshape(m, d)

    wqkv = p["attn_w_qkv"][0]
    nqk = ATTN_HEADS * 2 * ATTN_HEAD_DIM
    g0 = p["mixer_norm"][0]
    (q,) = norm_matmul(x, g0, wqkv, 0, nqk, ((BF16, "rows"),), tm, 512)
    k32, k16 = norm_matmul(x, g0, wqkv, nqk, nqk, ((F32, "interleaved"), (BF16, "rows")),
                           _pick(m, (512, 256)), nqk)
    v32, v16 = norm_matmul(x, g0, wqkv, 2 * nqk, nqk, ((F32, "rows"), (BF16, "rows")), tm, 512)
    sh = (nseq, L, nqk)
    lam_init = 0.8 - 0.6 * math.exp(-0.3 * 0)
    if is_prompt:
        o = attn_prompt(q.reshape(sh), k16.reshape(sh), v16.reshape(sh), p["slopes"], p["lam"],
                        p["attn_subln"][0], lam_init, tq=_pick(L, (256, 128, 64)),
                        tk=_pick(L, (512, 256, 128, 64)))
    else:
        past = cache_k.shape[2]
        o = attn_sample(q.reshape(sh), k16.reshape(sh), v16.reshape(sh),
                        cache_k[0].reshape(nseq, past * ATTN_HEADS * 2, ATTN_HEAD_DIM),
                        cache_v[0].reshape(nseq, past * ATTN_HEADS, ATTN_V_DIM),
                        p["slopes"], p["lam"], p["attn_subln"][0], lam_init,
                        tk=_pick(past, (512, 256, 128)))
    x = matmul_res(o.reshape(m, nqk), p["attn_w_o"][0], x, tm, 512)
    prev0 = jnp.zeros((nseq, CONV_W - 1, dff), F32) if is_prompt else state_conv[0]
    x3b, cs0 = conv_ffn(x.reshape(nseq, L, d), prev0, p["ffn_norm"][0], p["ffn_w_up"][0],
                        p["ffn_conv_w"][0], p["ffn_conv_b"][0], p["ffn_w_down"][0], None,
                        ffn_bs, ffn_lt, tf)
    x = x3b.reshape(m, d)

    wq = p["rec_w_qfig"][0]
    (qfig,) = norm_matmul(x, p["mixer_norm"][1], wq, 0, wq.shape[1], ((F32, "groups"),), tm, 512)
    if is_prompt:
        s0 = jnp.zeros((nseq, REC_HEADS, REC_DK, REC_DV), F32)
        chunk = _pick(L, (128, 64, 32, 16))
        hb, tbs, sub = 2, _pick(L, (512, 256, 128, 64, 32, 16)), 16
    else:
        s0 = state_hgrn[0]
        hb, chunk, tbs, sub = REC_HEADS, L, L, 16
    o, s_fin = hgrn(qfig.reshape(4, REC_HEADS, nseq, L, REC_DK), s0, p["lb"], p["rec_out_norm"][0],
                    hb, tbs, chunk, sub)
    x = matmul_res(o.reshape(m, d), p["rec_w_o"][0], x, tm, 512)
    prev1 = jnp.zeros((nseq, CONV_W - 1, dff), F32) if is_prompt else state_conv[1]
    y, cs1 = conv_ffn(x.reshape(nseq, L, d), prev1, p["ffn_norm"][1], p["ffn_w_up"][1],
                      p["ffn_conv_w"][1], p["ffn_conv_b"][1], p["ffn_w_down"][1], p["final_norm"],
                      ffn_bs, ffn_lt, tf)
    kshape = (1, nseq, L, ATTN_HEADS, 2, ATTN_HEAD_DIM)
    vshape = (1, nseq, L, ATTN_HEADS, ATTN_V_DIM)
    return (y, k32.reshape(kshape), v32.reshape(vshape), s_fin[None], jnp.stack([cs0, cs1]))


def kernel(x_prompt, x_sample, cache_k, cache_v, state_hgrn, state_conv, mixer_norm, ffn_norm,
           attn_w_qkv, attn_lambda_q1, attn_lambda_k1, attn_lambda_q2, attn_lambda_k2, attn_subln,
           attn_w_o, rec_w_qfig, rec_lower_bounds, rec_out_norm, rec_w_o, ffn_w_up, ffn_conv_w,
           ffn_conv_b, ffn_w_down, final_norm):
    lam_init = 0.8 - 0.6 * math.exp(-0.3 * 0)
    lam = (jnp.exp(jnp.sum(attn_lambda_q1[0] * attn_lambda_k1[0]))
           - jnp.exp(jnp.sum(attn_lambda_q2[0] * attn_lambda_k2[0])) + lam_init).reshape(1)
    slopes = jnp.exp2(-8.0 * jnp.arange(1, ATTN_HEADS + 1, dtype=F32) / ATTN_HEADS)
    lbs = jax.nn.softmax(rec_lower_bounds.astype(F32), axis=0)
    lb = (jnp.cumsum(lbs, axis=0) - lbs[0])[1]
    p = dict(
        mixer_norm=mixer_norm, ffn_norm=ffn_norm, attn_subln=attn_subln, rec_out_norm=rec_out_norm,
        ffn_conv_w=ffn_conv_w, ffn_conv_b=ffn_conv_b, final_norm=final_norm,
        lam=lam, slopes=slopes, lb=lb,
        attn_w_qkv=attn_w_qkv.astype(BF16), attn_w_o=attn_w_o.astype(BF16),
        rec_w_qfig=rec_w_qfig.astype(BF16), rec_w_o=rec_w_o.astype(BF16),
        ffn_w_up=ffn_w_up.astype(BF16), ffn_w_down=ffn_w_down.astype(BF16),
    )
    yp, kp, vp, sp, cp = _trunk(x_prompt, True, None, None, None, None, p)
    ys, ks, vs, ss, cs = _trunk(x_sample, False, cache_k, cache_v, state_hgrn, state_conv, p)
    return (yp, ys, kp, vp, ks, vs, sp, ss, cp, cs)
```

```python
import functools
import math

import jax
import jax.numpy as jnp
from jax import lax
from jax.experimental import pallas as pl
from jax.experimental.pallas import tpu as pltpu

F32 = jnp.float32
BF16 = jnp.bfloat16

CHUNK = 64
EPS = 1e-6
ATTN_HEADS = 8
ATTN_HEAD_DIM = 128
ATTN_V_DIM = 2 * ATTN_HEAD_DIM
ATTN_SCALE = ATTN_HEAD_DIM ** -0.5
LOG2E = math.log2(math.e)
REC_HEADS = 16
REC_DK = 128
REC_DV = 128
CONV_W = 3
NEG_BIG = -1e30
EXP_CLAMP = 80.0

V7X_VMEM_LIMIT = 56 * 1024 * 1024

_NT = (((1,), (1,)), ((), ()))
_TN = (((0,), (0,)), ((), ()))


def _params(*sem):
    return pltpu.CompilerParams(dimension_semantics=sem, vmem_limit_bytes=V7X_VMEM_LIMIT)


def _rms(x, g):
    ms = jnp.mean(x * x, axis=-1, keepdims=True)
    return (x * lax.rsqrt(ms + EPS)) * g


def _silu(x):
    return x / (1.0 + jnp.exp(-x))


LANES = 128


def _norm_matmul_kernel(x_ref, g_ref, w_ref, *rest, layouts, out_scale):
    outs, xn_ref = rest[:len(layouts)], rest[len(layouts)]

    @pl.when(pl.program_id(1) == 0)
    def _():
        xn_ref[...] = _rms(x_ref[...], g_ref[...]).astype(BF16)

    acc = jnp.dot(xn_ref[...], w_ref[...], preferred_element_type=F32)
    if out_scale is not None:
        acc = acc * out_scale
    tm, tn = acc.shape
    ngrp = tn // LANES
    for o, layout in zip(outs, layouts):
        if layout == "rows":
            o[...] = acc.astype(o.dtype)
        elif layout == "groups":
            for gi in range(ngrp):
                o[gi] = acc[:, gi * LANES:(gi + 1) * LANES].astype(o.dtype)
        else:
            for gi in range(ngrp):
                o[pl.ds(gi, tm, stride=ngrp), :] = acc[:, gi * LANES:(gi + 1) * LANES].astype(o.dtype)


def norm_matmul(x, g, w, col0, ncols, outs, tm, tn, out_scale=None):
    m, d = x.shape
    assert m % tm == 0 and ncols % tn == 0 and col0 % tn == 0 and tn % LANES == 0
    joff = col0 // tn
    ngrp = tn // LANES
    specs, shapes = [], []
    for dt, layout in outs:
        if layout == "rows":
            specs.append(pl.BlockSpec((tm, tn), lambda i, j: (i, j)))
            shapes.append(jax.ShapeDtypeStruct((m, ncols), dt))
        elif layout == "groups":
            specs.append(pl.BlockSpec((ngrp, tm, LANES), lambda i, j: (j, i, 0)))
            shapes.append(jax.ShapeDtypeStruct((ncols // LANES, m, LANES), dt))
        else:
            assert layout == "interleaved" and tn == ncols and dt == F32
            specs.append(pl.BlockSpec((tm * ngrp, LANES), lambda i, j: (i, 0)))
            shapes.append(jax.ShapeDtypeStruct((m * ngrp, LANES), dt))
    return pl.pallas_call(
        functools.partial(_norm_matmul_kernel, layouts=tuple(l for _, l in outs),
                          out_scale=out_scale),
        grid=(m // tm, ncols // tn),
        in_specs=[
            pl.BlockSpec((tm, d), lambda i, j: (i, 0)),
            pl.BlockSpec((1, d), lambda i, j: (0, 0)),
            pl.BlockSpec((d, tn), lambda i, j: (0, j + joff)),
        ],
        out_specs=specs,
        out_shape=shapes,
        scratch_shapes=[pltpu.VMEM((tm, d), BF16)],
        compiler_params=_params("parallel", "arbitrary"),
        name="norm_matmul",
    )(x, g.reshape(1, d), w)


def _matmul_res_kernel(a_ref, w_ref, r_ref, o_ref):
    o_ref[...] = r_ref[...] + jnp.dot(a_ref[...], w_ref[...], preferred_element_type=F32)


def matmul_res(a, w, res, tm, tn):
    m, k = a.shape
    n = w.shape[1]
    assert m % tm == 0 and n % tn == 0
    return pl.pallas_call(
        _matmul_res_kernel,
        grid=(m // tm, n // tn),
        in_specs=[
            pl.BlockSpec((tm, k), lambda i, j: (i, 0)),
            pl.BlockSpec((k, tn), lambda i, j: (0, j)),
            pl.BlockSpec((tm, tn), lambda i, j: (i, j)),
        ],
        out_specs=pl.BlockSpec((tm, tn), lambda i, j: (i, j)),
        out_shape=jax.ShapeDtypeStruct((m, n), F32),
        compiler_params=_params("parallel", "arbitrary"),
        name="matmul_res",
    )(a, w, res)


def _ffn_kernel(x_ref, prev_ref, g_ref, wg_ref, wv_ref, cw_ref, cb_ref, wd_ref, fg_ref,
                y_ref, cs_ref, xn_ref, ext_ref, carry_ref, *, final_norm, nsplit):
    l = pl.program_id(1)
    j = pl.program_id(2)
    nj = pl.num_programs(2)
    bs, lt, d = x_ref.shape
    tf = wg_ref.shape[1]
    rows = bs * lt
    halo = CONV_W - 1

    @pl.when(j == 0)
    def _():
        x = x_ref[...].reshape(rows, d)
        xn_ref[...] = _rms(x, g_ref[...]).astype(BF16)
        y_ref[...] = x_ref[...]

    @pl.when(l == 0)
    def _():
        ext_ref[0, :, 8 - halo:8, :] = prev_ref[...]

    @pl.when(l > 0)
    def _():
        ext_ref[0, :, 8 - halo:8, :] = carry_ref[j]

    lp = lt // nsplit
    rp = bs * lp
    last = None
    for s in range(nsplit):
        xn = xn_ref[s * rp:(s + 1) * rp, :]
        gate = jnp.dot(xn, wg_ref[...], preferred_element_type=F32).reshape(bs, lp, tf)
        val = jnp.dot(xn, wv_ref[...], preferred_element_type=F32).reshape(bs, lp, tf)
        if s > 0:
            ext_ref[s, :, 8 - halo:8, :] = last
        ext_ref[s, :, 8:8 + lp, :] = gate
        last = gate[:, lp - halo:lp, :]
        c = cb_ref[...].reshape(1, 1, tf)
        for t in range(CONV_W):
            off = 8 - halo + t
            c = c + ext_ref[s, :, off:off + lp, :] * cw_ref[t:t + 1, :].reshape(1, 1, tf)
        act = (_silu(c) * val).reshape(rp, tf).astype(BF16)
        upd = jnp.dot(act, wd_ref[...], preferred_element_type=F32)
        y_ref[:, s * lp:(s + 1) * lp, :] += upd.reshape(bs, lp, d)
    carry_ref[j] = last
    cs_ref[...] = last

    if final_norm:
        @pl.when(j == nj - 1)
        def _():
            y_ref[...] = _rms(y_ref[...], fg_ref[...].reshape(1, 1, d))


def conv_ffn(x3, prev, g, w_up, conv_w, conv_b, w_down, final_g, bs, lt, tf):
    nseq, L, d = x3.shape
    dff = w_down.shape[0]
    assert nseq % bs == 0 and L % lt == 0 and dff % tf == 0 and lt % 8 == 0
    nj = dff // tf
    halo = CONV_W - 1
    final_norm = final_g is not None
    fg = (final_g if final_norm else g).reshape(1, d)
    nsplit = 2 if (bs == 1 and lt % 256 == 0) else 1
    y, tails = pl.pallas_call(
        functools.partial(_ffn_kernel, final_norm=final_norm, nsplit=nsplit),
        grid=(nseq // bs, L // lt, nj),
        in_specs=[
            pl.BlockSpec((bs, lt, d), lambda s, l, j: (s, l, 0)),
            pl.BlockSpec((bs, halo, tf), lambda s, l, j: (s, 0, j)),
            pl.BlockSpec((1, d), lambda s, l, j: (0, 0)),
            pl.BlockSpec((d, tf), lambda s, l, j: (0, j)),
            pl.BlockSpec((d, tf), lambda s, l, j: (0, j + nj)),
            pl.BlockSpec((CONV_W, tf), lambda s, l, j: (0, j)),
            pl.BlockSpec((1, tf), lambda s, l, j: (0, j)),
            pl.BlockSpec((tf, d), lambda s, l, j: (j, 0)),
            pl.BlockSpec((1, d), lambda s, l, j: (0, 0)),
        ],
        out_specs=[
            pl.BlockSpec((bs, lt, d), lambda s, l, j: (s, l, 0)),
            pl.BlockSpec((bs, None, halo, tf), lambda s, l, j: (s, l, 0, j)),
        ],
        out_shape=[
            jax.ShapeDtypeStruct((nseq, L, d), F32),
            jax.ShapeDtypeStruct((nseq, L // lt, halo, dff), F32),
        ],
        scratch_shapes=[
            pltpu.VMEM((bs * lt, d), BF16),
            pltpu.VMEM((nsplit, bs, lt // nsplit + 8, tf), F32),
            pltpu.VMEM((nj, bs, halo, tf), F32),
        ],
        compiler_params=_params("parallel", "arbitrary", "arbitrary"),
        name="conv_ffn",
    )(x3, prev, g.reshape(1, d), w_up, w_up, conv_w, conv_b.reshape(1, dff), w_down, fg)
    return y, tails[:, -1]


def _softmax_step(s, m, l, acc, vb):
    m_new = jnp.maximum(m, jnp.max(s, axis=-1, keepdims=True))
    alpha = jnp.exp2(m - m_new)
    p = jnp.exp2(s - m_new)
    l_new = alpha * l + jnp.sum(p, axis=-1, keepdims=True)
    acc_new = alpha * acc + jnp.dot(p.astype(BF16), vb, preferred_element_type=F32)
    return m_new, l_new, acc_new


def _subln(o, g, lam_init):
    return _rms(o, g) * (1.0 - lam_init)


def _attn_prompt_kernel(slopes_ref, lam_ref, q_ref, k_ref, v_ref, g_ref, o_ref,
                        vt_ref, own_ref, p_ref, acc_ref, *, lam_init):
    h = pl.program_id(1)
    qi = pl.program_id(2)
    t = q_ref.shape[0]
    dh = ATTN_HEAD_DIM
    slope2 = slopes_ref[h] * LOG2E
    lam = lam_ref[0]
    q0 = qi * t

    @pl.when(qi == 0)
    def _():
        vt_ref[...] = v_ref[...].T
        k_io = lax.broadcasted_iota(jnp.int32, (t, t), 0)
        q_io = lax.broadcasted_iota(jnp.int32, (t, t), 1)
        own_ref[...] = jnp.where((k_io // CHUNK) <= (q_io // CHUNK),
                                 slope2 * (q_io - jnp.abs(q_io - k_io)).astype(F32), NEG_BIG)

    q = q_ref[...]

    def probs(c, kb, bias, m, l):
        s = lax.dot_general(kb[:, c * dh:(c + 1) * dh], q[:, c * dh:(c + 1) * dh], _NT,
                            preferred_element_type=F32) + bias
        m_new = jnp.maximum(m, jnp.max(s, axis=0, keepdims=True))
        alpha = jnp.exp2(m - m_new)
        p = jnp.exp2(s - m_new)
        return m_new, alpha * l + jnp.sum(p, axis=0, keepdims=True), alpha, p.astype(BF16)

    def add_values(c, k0, alpha, p):
        acc_ref[c] = alpha * acc_ref[c] + jnp.dot(vt_ref[:, pl.ds(k0, t)], p,
                                                  preferred_element_type=F32)

    def tile(k0, prev0, bias, state):
        (m0, l0), (m1, l1, alpha1) = state
        kb = k_ref[pl.ds(k0, t), :]
        m0, l0, alpha0, p0 = probs(0, kb, bias, m0, l0)
        if prev0 is not None:
            add_values(1, prev0, alpha1, p_ref[...])
        m1, l1, alpha1, p1 = probs(1, kb, bias, m1, l1)
        p_ref[...] = p1
        add_values(0, k0, alpha0, p0)
        return (m0, l0), (m1, l1, alpha1)

    neg, zero = jnp.full((1, t), NEG_BIG, F32), jnp.zeros((1, t), F32)
    acc_ref[...] = jnp.zeros(acc_ref.shape, F32)
    own0 = pl.multiple_of(q0, t)
    state = tile(own0, None, own_ref[...], ((neg, zero), (neg, zero, zero)))

    key = lax.broadcasted_iota(jnp.int32, (t, LANES), 0)

    def past(kt, state):
        k0 = pl.multiple_of(kt * t, t)
        prev0 = pl.multiple_of(jnp.where(kt == 0, q0, k0 - t), t)
        bias = slope2 * (key + (k0 - q0)).astype(F32)
        return tile(k0, prev0, jnp.concatenate([bias] * (t // LANES), axis=1), state)

    (_, l0), (_, l1, alpha1) = lax.fori_loop(0, qi, past, state)
    add_values(1, pl.multiple_of(jnp.where(qi == 0, q0, q0 - t), t), alpha1, p_ref[...])
    a0, a1 = acc_ref[0], acc_ref[1]
    o = (a0 / l0 - lam * (a1 / l1)).T
    o_ref[...] = _subln(o, g_ref[...], lam_init).astype(o_ref.dtype)


def attn_prompt(q, k, v, slopes, lam, subln, lam_init, tq):
    b, t, _ = q.shape
    hw = 2 * ATTN_HEAD_DIM
    assert t % tq == 0 and tq % CHUNK == 0
    smem = pl.BlockSpec(memory_space=pltpu.SMEM)
    return pl.pallas_call(
        functools.partial(_attn_prompt_kernel, lam_init=lam_init),
        grid=(b, ATTN_HEADS, t // tq),
        in_specs=[
            smem, smem,
            pl.BlockSpec((None, tq, hw), lambda b, h, i: (b, i, h)),
            pl.BlockSpec((None, t, hw), lambda b, h, i: (b, 0, h)),
            pl.BlockSpec((None, t, hw), lambda b, h, i: (b, 0, h)),
            pl.BlockSpec((1, hw), lambda b, h, i: (0, 0)),
        ],
        out_specs=pl.BlockSpec((None, tq, hw), lambda b, h, i: (b, i, h)),
        out_shape=jax.ShapeDtypeStruct(q.shape, BF16),
        scratch_shapes=[
            pltpu.VMEM((hw, t), BF16),
            pltpu.VMEM((tq, tq), F32),
            pltpu.VMEM((tq, tq), BF16),
            pltpu.VMEM((2, hw, tq), F32),
        ],
        compiler_params=_params("parallel", "parallel", "arbitrary"),
        name="attn_prompt",
    )(slopes, lam, q, k, v, subln.reshape(1, hw))


def _attn_sample_kernel(slopes_ref, lam_ref, q_ref, kc_ref, vlo_ref, vhi_ref, kn_ref, vn_ref, g_ref,
                        o_ref, m_ref, l_ref, acc_ref, *, past, lam_init):
    kt = pl.program_id(1)
    nkt = pl.num_programs(1)
    nq = q_ref.shape[0]
    tk = vlo_ref.shape[0] // ATTN_HEADS
    dh = ATTN_HEAD_DIM
    hw = 2 * dh
    lam = lam_ref[0]

    @pl.when(kt == 0)
    def _():
        m_ref[...] = jnp.full(m_ref.shape, NEG_BIG, F32)
        l_ref[...] = jnp.zeros(l_ref.shape, F32)
        acc_ref[...] = jnp.zeros(acc_ref.shape, F32)

    def scores(h, kb):
        qh = q_ref[:, h * hw:(h + 1) * hw]
        s0 = lax.dot_general(qh[:, :dh], kb[:, :dh], _NT, preferred_element_type=F32)
        s1 = lax.dot_general(qh[:, dh:], kb[:, dh:], _NT, preferred_element_type=F32)
        return jnp.concatenate([s0, s1], axis=0)

    def update(h, s, vb):
        m, l, acc = _softmax_step(s, m_ref[h][:, :1], l_ref[h][:, :1], acc_ref[h], vb)
        m_ref[h] = jnp.broadcast_to(m, m_ref.shape[1:])
        l_ref[h] = jnp.broadcast_to(l, l_ref.shape[1:])
        acc_ref[h] = acc

    col = (lax.broadcasted_iota(jnp.int32, (1, tk), 1) + (kt * tk - past)).astype(F32)
    for h in range(ATTN_HEADS):
        kb = jnp.concatenate(
            [kc_ref[pl.ds(2 * h + c, tk, stride=2 * ATTN_HEADS), :] for c in range(2)],
            axis=1).astype(BF16)
        vb = jnp.concatenate(
            [vlo_ref[pl.ds(h, tk, stride=ATTN_HEADS), :], vhi_ref[pl.ds(h, tk, stride=ATTN_HEADS), :]],
            axis=1).astype(BF16)
        update(h, scores(h, kb) + (slopes_ref[h] * LOG2E) * col, vb)

    @pl.when(kt == nkt - 1)
    def _():
        rn = lax.broadcasted_iota(jnp.int32, (2 * nq, nq), 0) % nq
        cn = lax.broadcasted_iota(jnp.int32, (2 * nq, nq), 1)
        rel_n = (rn - jnp.abs(rn - cn)).astype(F32)
        allowed = ((cn + past) // CHUNK) <= ((rn + past) // CHUNK)
        for h in range(ATTN_HEADS):
            kb = kn_ref[:, h * hw:(h + 1) * hw]
            vb = vn_ref[:, h * hw:(h + 1) * hw]
            s = jnp.where(allowed, scores(h, kb) + (slopes_ref[h] * LOG2E) * rel_n, NEG_BIG)
            update(h, s, vb)
            on = acc_ref[h] / l_ref[h][:, :1]
            o = on[:nq] - lam * on[nq:]
            o_ref[:, h * hw:(h + 1) * hw] = _subln(o, g_ref[...], lam_init).astype(o_ref.dtype)


def attn_sample(q, k_new, v_new, k_cache, v_cache, slopes, lam, subln, lam_init, tk):
    b, nq, d = q.shape
    nh, dh = ATTN_HEADS, ATTN_HEAD_DIM
    past = v_cache.shape[1] // nh
    hw = 2 * dh
    assert past % tk == 0 and past > 0 and dh == LANES
    smem = pl.BlockSpec(memory_space=pltpu.SMEM)
    return pl.pallas_call(
        functools.partial(_attn_sample_kernel, past=past, lam_init=lam_init),
        grid=(b, past // tk),
        in_specs=[
            smem, smem,
            pl.BlockSpec((None, nq, d), lambda b, k: (b, 0, 0)),
            pl.BlockSpec((None, tk * nh * 2, dh), lambda b, k: (b, k, 0)),
            pl.BlockSpec((None, tk * nh, dh), lambda b, k: (b, k, 0)),
            pl.BlockSpec((None, tk * nh, dh), lambda b, k: (b, k, 1)),
            pl.BlockSpec((None, nq, d), lambda b, k: (b, 0, 0)),
            pl.BlockSpec((None, nq, d), lambda b, k: (b, 0, 0)),
            pl.BlockSpec((1, hw), lambda b, k: (0, 0)),
        ],
        out_specs=pl.BlockSpec((None, nq, d), lambda b, k: (b, 0, 0)),
        out_shape=jax.ShapeDtypeStruct(q.shape, BF16),
        scratch_shapes=[
            pltpu.VMEM((ATTN_HEADS, 2 * nq, 128), F32),
            pltpu.VMEM((ATTN_HEADS, 2 * nq, 128), F32),
            pltpu.VMEM((ATTN_HEADS, 2 * nq, hw), F32),
        ],
        compiler_params=_params("parallel", "arbitrary"),
        name="attn_sample",
    )(slopes, lam, q, k_cache, v_cache, v_cache, k_new, v_new, subln.reshape(1, hw))


def _cumsum_rows(tri, x):
    hi = x.astype(BF16)
    r1 = x - hi.astype(F32)
    mid = r1.astype(BF16)
    lo = (r1 - mid.astype(F32)).astype(BF16)
    n = x.shape[1]
    y = jnp.dot(tri, jnp.concatenate([hi, mid, lo], axis=1), preferred_element_type=F32)
    return y[:, :n] + y[:, n:2 * n] + y[:, 2 * n:]


def _hgrn_kernel(q_ref, z_ref, i_ref, g_ref, lb_ref, gn_ref, s0_ref, o_ref, sout_ref,
                 s_ref, qb_ref, kb_ref, *, chunk, sub):
    tb = pl.program_id(2)
    hb, tbs, dk = q_ref.shape
    nch = tbs // chunk
    nsub = chunk // sub

    @pl.when(tb == 0)
    def _():
        s_ref[...] = s0_ref[...]
        qb_ref[...] = jnp.zeros(qb_ref.shape, BF16)
        kb_ref[...] = jnp.zeros(kb_ref.shape, BF16)

    r_io = lax.broadcasted_iota(jnp.int32, (chunk, chunk), 0)
    c_io = lax.broadcasted_iota(jnp.int32, (chunk, chunk), 1)
    causal = c_io <= r_io
    tri = causal.astype(BF16)

    for h in range(hb):
        one_m_lb = 1.0 - lb_ref[h]
        parts = []
        for c in range(nch):
            t0, slot = c * chunk, h * nch + c
            z = z_ref[h, t0:t0 + chunk, :]
            kk = one_m_lb / (1.0 + jnp.exp(z))
            b = _cumsum_rows(tri, jnp.log1p(-kk))
            qs = _silu(q_ref[h, t0:t0 + chunk, :])
            v = i_ref[h, t0:t0 + chunk, :].astype(BF16)
            b_last = b[chunk - 1:chunk, :]
            for i in range(nsub):
                lo, hi = sub * i, sub * (i + 1)
                ref = b[lo - 1:lo, :] if i > 0 else jnp.zeros((1, dk), F32)
                qb_ref[slot, lo:hi, dk * i:dk * (i + 1)] = (
                    qs[lo:hi] * jnp.exp(b[lo:hi] - ref)).astype(BF16)
                kb_ref[slot, 0:hi, dk * i:dk * (i + 1)] = (
                    kk[:hi] * jnp.exp(jnp.minimum(ref - b[:hi], EXP_CLAMP))).astype(BF16)
            a = lax.dot_general(qb_ref[slot], kb_ref[slot], _NT, preferred_element_type=F32)
            a = jnp.where(causal, a, 0.0).astype(BF16)
            o_intra = jnp.dot(a, v, preferred_element_type=F32)
            q_end = (qs * jnp.exp(b)).astype(BF16)
            k_end = (kk * jnp.exp(b_last - b)).astype(BF16)
            upd = lax.dot_general(k_end, v, _TN, preferred_element_type=F32)
            decay = jnp.transpose(jnp.broadcast_to(jnp.exp(b_last), (REC_DV, dk)))
            parts.append((o_intra, q_end, decay, upd))
        state = s_ref[h]
        for c in range(nch):
            t0 = c * chunk
            o_intra, q_end, decay, upd = parts[c]
            o = o_intra + jnp.dot(q_end, state.astype(BF16), preferred_element_type=F32)
            state = decay * state + upd
            y = _rms(o, gn_ref[...]) * _silu(g_ref[h, t0:t0 + chunk, :])
            o_ref[t0:t0 + chunk, h * REC_DV:(h + 1) * REC_DV] = y.astype(o_ref.dtype)
        s_ref[h] = state
    sout_ref[...] = s_ref[...]


def hgrn(qfig, s0, lb, out_norm, hb, tbs, chunk, sub):
    _, nh, b, t, dk = qfig.shape
    dv = REC_DV
    assert t % tbs == 0 and tbs % chunk == 0 and chunk % sub == 0 and sub % 16 == 0 and nh % hb == 0
    nsub = chunk // sub
    nslot = hb * (tbs // chunk)

    def kind(k):
        return pl.BlockSpec((None, hb, None, tbs, dk), lambda b, h, i: (k, h, b, i, 0))

    return pl.pallas_call(
        functools.partial(_hgrn_kernel, chunk=chunk, sub=sub),
        grid=(b, nh // hb, t // tbs),
        in_specs=[
            kind(0), kind(1), kind(2), kind(3),
            pl.BlockSpec((hb, 1, dk), lambda b, h, i: (h, 0, 0)),
            pl.BlockSpec((1, dv), lambda b, h, i: (0, 0)),
            pl.BlockSpec((None, hb, dk, dv), lambda b, h, i: (b, h, 0, 0)),
        ],
        out_specs=[
            pl.BlockSpec((None, tbs, hb * dv), lambda b, h, i: (b, i, h)),
            pl.BlockSpec((None, hb, dk, dv), lambda b, h, i: (b, h, 0, 0)),
        ],
        out_shape=[
            jax.ShapeDtypeStruct((b, t, nh * dv), BF16),
            jax.ShapeDtypeStruct((b, nh, dk, dv), F32),
        ],
        scratch_shapes=[
            pltpu.VMEM((hb, dk, dv), F32),
            pltpu.VMEM((nslot, chunk, nsub * dk), BF16),
            pltpu.VMEM((nslot, chunk, nsub * dk), BF16),
        ],
        compiler_params=_params("parallel", "parallel", "arbitrary"),
        name="hgrn",
    )(qfig, qfig, qfig, qfig, lb.reshape(nh, 1, dk), out_norm.reshape(1, dv), s0)


def _pick(n, candidates):
    for c in candidates:
        if n % c == 0:
            return c
    return n


def _trunk(x3, is_prompt, cache_k, cache_v, state_hgrn, state_conv, p):
    nseq, L, d = x3.shape
    m = nseq * L
    dff = p["ffn_w_down"].shape[1]
    tm = _pick(m, (1024, 512, 256))
    tf = _pick(dff, (512, 256, 128))
    if is_prompt:
        ffn_bs, ffn_lt = 1, _pick(L, (512, 256, 128))
    else:
        ffn_bs, ffn_lt = nseq, L
    x = x3.reshape(m, d)

    wqkv = p["attn_w_qkv"][0]
    nqk = ATTN_HEADS * 2 * ATTN_HEAD_DIM
    g0 = p["mixer_norm"][0]
    (q,) = norm_matmul(x, g0, wqkv, 0, nqk, ((BF16, "rows"),), tm, 512,
                       out_scale=ATTN_SCALE * LOG2E)
    k32, k16 = norm_matmul(x, g0, wqkv, nqk, nqk, ((F32, "interleaved"), (BF16, "rows")),
                           _pick(m, (512, 256)), nqk)
    v32, v16 = norm_matmul(x, g0, wqkv, 2 * nqk, nqk, ((F32, "rows"), (BF16, "rows")), tm, 512)
    sh = (nseq, L, nqk)
    lam_init = 0.8 - 0.6 * math.exp(-0.3 * 0)
    if is_prompt:
        o = attn_prompt(q.reshape(sh), k16.reshape(sh), v16.reshape(sh), p["slopes"], p["lam"],
                        p["attn_subln"][0], lam_init, tq=_pick(L, (512, 256, 128, 64)))
    else:
        past = cache_k.shape[2]
        o = attn_sample(q.reshape(sh), k16.reshape(sh), v16.reshape(sh),
                        cache_k[0].reshape(nseq, past * ATTN_HEADS * 2, ATTN_HEAD_DIM),
                        cache_v[0].reshape(nseq, past * ATTN_HEADS, ATTN_V_DIM),
                        p["slopes"], p["lam"], p["attn_subln"][0], lam_init,
                        tk=_pick(past, (512, 256, 128)))
    x = matmul_res(o.reshape(m, nqk), p["attn_w_o"][0], x, tm, 512)
    prev0 = jnp.zeros((nseq, CONV_W - 1, dff), F32) if is_prompt else state_conv[0]
    x3b, cs0 = conv_ffn(x.reshape(nseq, L, d), prev0, p["ffn_norm"][0], p["ffn_w_up"][0],
                        p["ffn_conv_w"][0], p["ffn_conv_b"][0], p["ffn_w_down"][0], None,
                        ffn_bs, ffn_lt, tf)
    x = x3b.reshape(m, d)

    wq = p["rec_w_qfig"][0]
    (qfig,) = norm_matmul(x, p["mixer_norm"][1], wq, 0, wq.shape[1], ((F32, "groups"),), tm, 512)
    if is_prompt:
        s0 = jnp.zeros((nseq, REC_HEADS, REC_DK, REC_DV), F32)
        chunk = _pick(L, (128, 64, 32, 16))
        hb, tbs, sub = 2, _pick(L, (512, 256, 128, 64, 32, 16)), 16
    else:
        s0 = state_hgrn[0]
        hb, chunk, tbs, sub = REC_HEADS, L, L, 16
    o, s_fin = hgrn(qfig.reshape(4, REC_HEADS, nseq, L, REC_DK), s0, p["lb"], p["rec_out_norm"][0],
                    hb, tbs, chunk, sub)
    x = matmul_res(o.reshape(m, d), p["rec_w_o"][0], x, tm, 512)
    prev1 = jnp.zeros((nseq, CONV_W - 1, dff), F32) if is_prompt else state_conv[1]
    y, cs1 = conv_ffn(x.reshape(nseq, L, d), prev1, p["ffn_norm"][1], p["ffn_w_up"][1],
                      p["ffn_conv_w"][1], p["ffn_conv_b"][1], p["ffn_w_down"][1], p["final_norm"],
                      ffn_bs, ffn_lt, tf)
    kshape = (1, nseq, L, ATTN_HEADS, 2, ATTN_HEAD_DIM)
    vshape = (1, nseq, L, ATTN_HEADS, ATTN_V_DIM)
    return (y, k32.reshape(kshape), v32.reshape(vshape), s_fin[None], jnp.stack([cs0, cs1]))


def kernel(x_prompt, x_sample, cache_k, cache_v, state_hgrn, state_conv, mixer_norm, ffn_norm,
           attn_w_qkv, attn_lambda_q1, attn_lambda_k1, attn_lambda_q2, attn_lambda_k2, attn_subln,
           attn_w_o, rec_w_qfig, rec_lower_bounds, rec_out_norm, rec_w_o, ffn_w_up, ffn_conv_w,
           ffn_conv_b, ffn_w_down, final_norm):
    lam_init = 0.8 - 0.6 * math.exp(-0.3 * 0)
    lam = (jnp.exp(jnp.sum(attn_lambda_q1[0] * attn_lambda_k1[0]))
           - jnp.exp(jnp.sum(attn_lambda_q2[0] * attn_lambda_k2[0])) + lam_init).reshape(1)
    slopes = jnp.exp2(-8.0 * jnp.arange(1, ATTN_HEADS + 1, dtype=F32) / ATTN_HEADS)
    lbs = jax.nn.softmax(rec_lower_bounds.astype(F32), axis=0)
    lb = (jnp.cumsum(lbs, axis=0) - lbs[0])[1]
    p = dict(
        mixer_norm=mixer_norm, ffn_norm=ffn_norm, attn_subln=attn_subln, rec_out_norm=rec_out_norm,
        ffn_conv_w=ffn_conv_w, ffn_conv_b=ffn_conv_b, final_norm=final_norm,
        lam=lam, slopes=slopes, lb=lb,
        attn_w_qkv=attn_w_qkv.astype(BF16), attn_w_o=attn_w_o.astype(BF16),
        rec_w_qfig=rec_w_qfig.astype(BF16), rec_w_o=rec_w_o.astype(BF16),
        ffn_w_up=ffn_w_up.astype(BF16), ffn_w_down=ffn_w_down.astype(BF16),
    )
    yp, kp, vp, sp, cp = _trunk(x_prompt, True, None, None, None, None, p)
    ys, ks, vs, ss, cs = _trunk(x_sample, False, cache_k, cache_v, state_hgrn, state_conv, p)
    return (yp, ys, kp, vp, ks, vs, sp, ss, cp, cs)
```

```python
import functools
import math

import jax
import jax.numpy as jnp
from jax import lax
from jax.experimental import pallas as pl
from jax.experimental.pallas import tpu as pltpu

F32 = jnp.float32
BF16 = jnp.bfloat16

CHUNK = 64
EPS = 1e-6
ATTN_HEADS = 8
ATTN_HEAD_DIM = 128
ATTN_V_DIM = 2 * ATTN_HEAD_DIM
ATTN_SCALE = ATTN_HEAD_DIM ** -0.5
LOG2E = math.log2(math.e)
REC_HEADS = 16
REC_DK = 128
REC_DV = 128
CONV_W = 3
NEG_BIG = -1e30
EXP_CLAMP = 80.0

V7X_VMEM_LIMIT = 56 * 1024 * 1024

_NT = (((1,), (1,)), ((), ()))
_TN = (((0,), (0,)), ((), ()))


def _params(*sem):
    return pltpu.CompilerParams(dimension_semantics=sem, vmem_limit_bytes=V7X_VMEM_LIMIT)


def _rms(x, g):
    ms = jnp.mean(x * x, axis=-1, keepdims=True)
    return (x * lax.rsqrt(ms + EPS)) * g


def _silu(x):
    return x / (1.0 + jnp.exp(-x))


LANES = 128


def _norm_matmul_kernel(x_ref, g_ref, w_ref, *rest, layouts, out_scale, gate_tiles):
    if gate_tiles:
        lb_ref, rest = rest[0], rest[1:]
    outs, xn_ref = rest[:len(layouts)], rest[len(layouts)]
    j = pl.program_id(1)

    @pl.when(j == 0)
    def _():
        xn_ref[...] = _rms(x_ref[...], g_ref[...]).astype(BF16)

    acc = jnp.dot(xn_ref[...], w_ref[...], preferred_element_type=F32)
    if out_scale is not None:
        acc = acc * out_scale
    tm, tn = acc.shape
    ngrp = tn // LANES

    def emit(val):
        for o, layout in zip(outs, layouts):
            if layout == "rows":
                o[...] = val.astype(o.dtype)
            elif layout == "groups":
                for gi in range(ngrp):
                    o[gi] = val[:, gi * LANES:(gi + 1) * LANES].astype(o.dtype)
            else:
                for gi in range(ngrp):
                    o[pl.ds(gi, tm, stride=ngrp), :] = val[:, gi * LANES:(gi + 1) * LANES].astype(o.dtype)

    if not gate_tiles:
        emit(acc)
    else:
        kind = j // gate_tiles

        @pl.when((kind == 0) | (kind == 3))
        def _():
            emit(_silu(acc))

        @pl.when(kind == 1)
        def _():
            emit((1.0 - lb_ref[...]) / (1.0 + jnp.exp(acc)))

        @pl.when(kind == 2)
        def _():
            emit(acc)


def norm_matmul(x, g, w, col0, ncols, outs, tm, tn, out_scale=None, hgrn_lb=None):
    m, d = x.shape
    assert m % tm == 0 and ncols % tn == 0 and col0 % tn == 0 and tn % LANES == 0
    joff = col0 // tn
    ngrp = tn // LANES
    extra_in, extra_specs, gate_tiles = [], [], 0
    if hgrn_lb is not None:
        quarter = ncols // 4
        assert quarter % tn == 0 and col0 == 0
        gate_tiles = quarter // tn
        zeros = jnp.zeros((quarter,), F32)
        extra_in = [jnp.concatenate([zeros, hgrn_lb, zeros, zeros]).reshape(1, ncols)]
        extra_specs = [pl.BlockSpec((1, tn), lambda i, j: (0, j))]
    specs, shapes = [], []
    for dt, layout in outs:
        if layout == "rows":
            specs.append(pl.BlockSpec((tm, tn), lambda i, j: (i, j)))
            shapes.append(jax.ShapeDtypeStruct((m, ncols), dt))
        elif layout == "groups":
            specs.append(pl.BlockSpec((ngrp, tm, LANES), lambda i, j: (j, i, 0)))
            shapes.append(jax.ShapeDtypeStruct((ncols // LANES, m, LANES), dt))
        else:
            assert layout == "interleaved" and tn == ncols and dt == F32
            specs.append(pl.BlockSpec((tm * ngrp, LANES), lambda i, j: (i, 0)))
            shapes.append(jax.ShapeDtypeStruct((m * ngrp, LANES), dt))
    return pl.pallas_call(
        functools.partial(_norm_matmul_kernel, layouts=tuple(l for _, l in outs),
                          out_scale=out_scale, gate_tiles=gate_tiles),
        grid=(m // tm, ncols // tn),
        in_specs=[
            pl.BlockSpec((tm, d), lambda i, j: (i, 0)),
            pl.BlockSpec((1, d), lambda i, j: (0, 0)),
            pl.BlockSpec((d, tn), lambda i, j: (0, j + joff)),
        ] + extra_specs,
        out_specs=specs,
        out_shape=shapes,
        scratch_shapes=[pltpu.VMEM((tm, d), BF16)],
        compiler_params=_params("parallel", "arbitrary"),
        name="norm_matmul",
    )(x, g.reshape(1, d), w, *extra_in)


def _matmul_res_kernel(a_ref, w_ref, r_ref, g_ref, o_ref, on_ref):
    out = r_ref[...] + jnp.dot(a_ref[...], w_ref[...], preferred_element_type=F32)
    o_ref[...] = out
    on_ref[...] = _rms(out, g_ref[...]).astype(on_ref.dtype)


def matmul_res(a, w, res, g, tm):
    m, k = a.shape
    n = w.shape[1]
    assert m % tm == 0
    return pl.pallas_call(
        _matmul_res_kernel,
        grid=(m // tm,),
        in_specs=[
            pl.BlockSpec((tm, k), lambda i: (i, 0)),
            pl.BlockSpec((k, n), lambda i: (0, 0)),
            pl.BlockSpec((tm, n), lambda i: (i, 0)),
            pl.BlockSpec((1, n), lambda i: (0, 0)),
        ],
        out_specs=[pl.BlockSpec((tm, n), lambda i: (i, 0)), pl.BlockSpec((tm, n), lambda i: (i, 0))],
        out_shape=[jax.ShapeDtypeStruct((m, n), F32), jax.ShapeDtypeStruct((m, n), BF16)],
        compiler_params=_params("parallel"),
        name="matmul_res",
    )(a, w, res, g.reshape(1, n))


def _ffn_kernel(x_ref, xn_ref, prev_ref, wg_ref, wv_ref, cw_ref, cb_ref, wd_ref, *rest,
                final_norm, nsplit):
    if final_norm:
        fg_ref, rest = rest[0], rest[1:]
    y_ref, cs_ref, ext_ref, carry_ref = rest
    l = pl.program_id(1)
    j = pl.program_id(2)
    nj = pl.num_programs(2)
    bs, lt, d = x_ref.shape
    tf = wg_ref.shape[1]
    halo = CONV_W - 1

    @pl.when(j == 0)
    def _():
        y_ref[...] = x_ref[...]

    @pl.when(l == 0)
    def _():
        ext_ref[0, :, 8 - halo:8, :] = prev_ref[...]

    @pl.when(l > 0)
    def _():
        ext_ref[0, :, 8 - halo:8, :] = carry_ref[j]

    lp = lt // nsplit
    rp = bs * lp
    last = None
    for s in range(nsplit):
        xn = xn_ref[:, s * lp:(s + 1) * lp, :].reshape(rp, d)
        gate = jnp.dot(xn, wg_ref[...], preferred_element_type=F32).reshape(bs, lp, tf)
        val = jnp.dot(xn, wv_ref[...], preferred_element_type=F32).reshape(bs, lp, tf)
        if s > 0:
            ext_ref[s, :, 8 - halo:8, :] = last
        ext_ref[s, :, 8:8 + lp, :] = gate
        last = gate[:, lp - halo:lp, :]
        c = cb_ref[...].reshape(1, 1, tf)
        for t in range(CONV_W):
            off = 8 - halo + t
            c = c + ext_ref[s, :, off:off + lp, :] * cw_ref[t:t + 1, :].reshape(1, 1, tf)
        act = (_silu(c) * val).reshape(rp, tf).astype(BF16)
        upd = jnp.dot(act, wd_ref[...], preferred_element_type=F32)
        y_ref[:, s * lp:(s + 1) * lp, :] += upd.reshape(bs, lp, d)
    carry_ref[j] = last
    cs_ref[...] = last

    if final_norm:
        @pl.when(j == nj - 1)
        def _():
            y_ref[...] = _rms(y_ref[...], fg_ref[...].reshape(1, 1, d))


def conv_ffn(x3, xn3, prev, w_up, conv_w, conv_b, w_down, final_g, bs, lt, tf):
    nseq, L, d = x3.shape
    dff = w_down.shape[0]
    assert nseq % bs == 0 and L % lt == 0 and dff % tf == 0 and lt % 8 == 0
    nj = dff // tf
    halo = CONV_W - 1
    final_norm = final_g is not None
    fg_in = [final_g.reshape(1, d)] if final_norm else []
    fg_spec = [pl.BlockSpec((1, d), lambda s, l, j: (0, 0))] if final_norm else []
    nsplit = 2 if (bs == 1 and lt % 256 == 0) else 1
    y, tails = pl.pallas_call(
        functools.partial(_ffn_kernel, final_norm=final_norm, nsplit=nsplit),
        grid=(nseq // bs, L // lt, nj),
        in_specs=[
            pl.BlockSpec((bs, lt, d), lambda s, l, j: (s, l, 0)),
            pl.BlockSpec((bs, lt, d), lambda s, l, j: (s, l, 0)),
            pl.BlockSpec((bs, halo, tf), lambda s, l, j: (s, 0, j)),
            pl.BlockSpec((d, tf), lambda s, l, j: (0, j)),
            pl.BlockSpec((d, tf), lambda s, l, j: (0, j + nj)),
            pl.BlockSpec((CONV_W, tf), lambda s, l, j: (0, j)),
            pl.BlockSpec((1, tf), lambda s, l, j: (0, j)),
            pl.BlockSpec((tf, d), lambda s, l, j: (j, 0)),
        ] + fg_spec,
        out_specs=[
            pl.BlockSpec((bs, lt, d), lambda s, l, j: (s, l, 0)),
            pl.BlockSpec((bs, None, halo, tf), lambda s, l, j: (s, l, 0, j)),
        ],
        out_shape=[
            jax.ShapeDtypeStruct((nseq, L, d), F32),
            jax.ShapeDtypeStruct((nseq, L // lt, halo, dff), F32),
        ],
        scratch_shapes=[
            pltpu.VMEM((nsplit, bs, lt // nsplit + 8, tf), F32),
            pltpu.VMEM((nj, bs, halo, tf), F32),
        ],
        compiler_params=_params("parallel", "arbitrary", "arbitrary"),
        name="conv_ffn",
    )(x3, xn3, prev, w_up, w_up, conv_w, conv_b.reshape(1, dff), w_down, *fg_in)
    return y, tails[:, -1]


def _softmax_step(s, m, l, acc, vb):
    m_new = jnp.maximum(m, jnp.max(s, axis=-1, keepdims=True))
    alpha = jnp.exp2(m - m_new)
    p = jnp.exp2(s - m_new)
    l_new = alpha * l + jnp.sum(p, axis=-1, keepdims=True)
    acc_new = alpha * acc + jnp.dot(p.astype(BF16), vb, preferred_element_type=F32)
    return m_new, l_new, acc_new


def _subln(o, g, lam_init):
    return _rms(o, g) * (1.0 - lam_init)


def _attn_prompt_kernel(slopes_ref, lam_ref, q_ref, k_ref, v_ref, g_ref, o_ref,
                        vt_ref, own_ref, p_ref, acc_ref, *, lam_init):
    h = pl.program_id(1)
    qi = pl.program_id(2)
    t = q_ref.shape[0]
    dh = ATTN_HEAD_DIM
    slope2 = slopes_ref[h] * LOG2E
    lam = lam_ref[0]
    q0 = qi * t

    @pl.when(qi == 0)
    def _():
        vt_ref[...] = v_ref[...].T
        k_io = lax.broadcasted_iota(jnp.int32, (t, t), 0)
        q_io = lax.broadcasted_iota(jnp.int32, (t, t), 1)
        own_ref[...] = jnp.where((k_io // CHUNK) <= (q_io // CHUNK),
                                 slope2 * (q_io - jnp.abs(q_io - k_io)).astype(F32), NEG_BIG)

    q = q_ref[...]

    def probs(c, kb, bias, m, l):
        s = lax.dot_general(kb[:, c * dh:(c + 1) * dh], q[:, c * dh:(c + 1) * dh], _NT,
                            preferred_element_type=F32) + bias
        m_new = jnp.maximum(m, jnp.max(s, axis=0, keepdims=True))
        alpha = jnp.exp2(m - m_new)
        p = jnp.exp2(s - m_new)
        return m_new, alpha * l + jnp.sum(p, axis=0, keepdims=True), alpha, p.astype(BF16)

    def add_values(c, k0, alpha, p):
        acc_ref[c] = alpha * acc_ref[c] + jnp.dot(vt_ref[:, pl.ds(k0, t)], p,
                                                  preferred_element_type=F32)

    def tile(k0, prev0, bias, state):
        (m0, l0), (m1, l1, alpha1) = state
        kb = k_ref[pl.ds(k0, t), :]
        m0, l0, alpha0, p0 = probs(0, kb, bias, m0, l0)
        if prev0 is not None:
            add_values(1, prev0, alpha1, p_ref[...])
        m1, l1, alpha1, p1 = probs(1, kb, bias, m1, l1)
        p_ref[...] = p1
        add_values(0, k0, alpha0, p0)
        return (m0, l0), (m1, l1, alpha1)

    neg, zero = jnp.full((1, t), NEG_BIG, F32), jnp.zeros((1, t), F32)
    acc_ref[...] = jnp.zeros(acc_ref.shape, F32)
    own0 = pl.multiple_of(q0, t)
    state = tile(own0, None, own_ref[...], ((neg, zero), (neg, zero, zero)))

    key = lax.broadcasted_iota(jnp.int32, (t, LANES), 0)

    def past(kt, state):
        k0 = pl.multiple_of(kt * t, t)
        prev0 = pl.multiple_of(jnp.where(kt == 0, q0, k0 - t), t)
        bias = slope2 * (key + (k0 - q0)).astype(F32)
        return tile(k0, prev0, jnp.concatenate([bias] * (t // LANES), axis=1), state)

    (_, l0), (_, l1, alpha1) = lax.fori_loop(0, qi, past, state)
    add_values(1, pl.multiple_of(jnp.where(qi == 0, q0, q0 - t), t), alpha1, p_ref[...])
    a0, a1 = acc_ref[0], acc_ref[1]
    o = (a0 / l0 - lam * (a1 / l1)).T
    o_ref[...] = _subln(o, g_ref[...], lam_init).astype(o_ref.dtype)


def attn_prompt(q, k, v, slopes, lam, subln, lam_init, tq):
    b, t, _ = q.shape
    hw = 2 * ATTN_HEAD_DIM
    assert t % tq == 0 and tq % CHUNK == 0
    smem = pl.BlockSpec(memory_space=pltpu.SMEM)
    return pl.pallas_call(
        functools.partial(_attn_prompt_kernel, lam_init=lam_init),
        grid=(b, ATTN_HEADS, t // tq),
        in_specs=[
            smem, smem,
            pl.BlockSpec((None, tq, hw), lambda b, h, i: (b, i, h)),
            pl.BlockSpec((None, t, hw), lambda b, h, i: (b, 0, h)),
            pl.BlockSpec((None, t, hw), lambda b, h, i: (b, 0, h)),
            pl.BlockSpec((1, hw), lambda b, h, i: (0, 0)),
        ],
        out_specs=pl.BlockSpec((None, tq, hw), lambda b, h, i: (b, i, h)),
        out_shape=jax.ShapeDtypeStruct(q.shape, BF16),
        scratch_shapes=[
            pltpu.VMEM((hw, t), BF16),
            pltpu.VMEM((tq, tq), F32),
            pltpu.VMEM((tq, tq), BF16),
            pltpu.VMEM((2, hw, tq), F32),
        ],
        compiler_params=_params("parallel", "parallel", "arbitrary"),
        name="attn_prompt",
    )(slopes, lam, q, k, v, subln.reshape(1, hw))


def _attn_sample_kernel(slopes_ref, lam_ref, q_ref, kc_ref, vlo_ref, vhi_ref, kn_ref, vn_ref, g_ref,
                        o_ref, m_ref, l_ref, acc_ref, *, past, lam_init):
    kt = pl.program_id(1)
    nkt = pl.num_programs(1)
    nq = q_ref.shape[0]
    tk = vlo_ref.shape[0] // ATTN_HEADS
    dh = ATTN_HEAD_DIM
    hw = 2 * dh
    lam = lam_ref[0]

    @pl.when(kt == 0)
    def _():
        m_ref[...] = jnp.full(m_ref.shape, NEG_BIG, F32)
        l_ref[...] = jnp.zeros(l_ref.shape, F32)
        acc_ref[...] = jnp.zeros(acc_ref.shape, F32)

    def scores(h, kb):
        qh = q_ref[:, h * hw:(h + 1) * hw]
        s0 = lax.dot_general(qh[:, :dh], kb[:, :dh], _NT, preferred_element_type=F32)
        s1 = lax.dot_general(qh[:, dh:], kb[:, dh:], _NT, preferred_element_type=F32)
        return jnp.concatenate([s0, s1], axis=0)

    def update(h, s, vb):
        m, l, acc = _softmax_step(s, m_ref[h][:, :1], l_ref[h][:, :1], acc_ref[h], vb)
        m_ref[h] = jnp.broadcast_to(m, m_ref.shape[1:])
        l_ref[h] = jnp.broadcast_to(l, l_ref.shape[1:])
        acc_ref[h] = acc

    col = (lax.broadcasted_iota(jnp.int32, (1, tk), 1) + (kt * tk - past)).astype(F32)
    for h in range(ATTN_HEADS):
        kb = jnp.concatenate(
            [kc_ref[pl.ds(2 * h + c, tk, stride=2 * ATTN_HEADS), :] for c in range(2)],
            axis=1).astype(BF16)
        vb = jnp.concatenate(
            [vlo_ref[pl.ds(h, tk, stride=ATTN_HEADS), :], vhi_ref[pl.ds(h, tk, stride=ATTN_HEADS), :]],
            axis=1).astype(BF16)
        update(h, scores(h, kb) + (slopes_ref[h] * LOG2E) * col, vb)

    @pl.when(kt == nkt - 1)
    def _():
        rn = lax.broadcasted_iota(jnp.int32, (2 * nq, nq), 0) % nq
        cn = lax.broadcasted_iota(jnp.int32, (2 * nq, nq), 1)
        rel_n = (rn - jnp.abs(rn - cn)).astype(F32)
        allowed = ((cn + past) // CHUNK) <= ((rn + past) // CHUNK)
        for h in range(ATTN_HEADS):
            kb = kn_ref[:, h * hw:(h + 1) * hw]
            vb = vn_ref[:, h * hw:(h + 1) * hw]
            s = jnp.where(allowed, scores(h, kb) + (slopes_ref[h] * LOG2E) * rel_n, NEG_BIG)
            update(h, s, vb)
            on = acc_ref[h] / l_ref[h][:, :1]
            o = on[:nq] - lam * on[nq:]
            o_ref[:, h * hw:(h + 1) * hw] = _subln(o, g_ref[...], lam_init).astype(o_ref.dtype)


def attn_sample(q, k_new, v_new, k_cache, v_cache, slopes, lam, subln, lam_init, tk):
    b, nq, d = q.shape
    nh, dh = ATTN_HEADS, ATTN_HEAD_DIM
    past = v_cache.shape[1] // nh
    hw = 2 * dh
    assert past % tk == 0 and past > 0 and dh == LANES
    smem = pl.BlockSpec(memory_space=pltpu.SMEM)
    return pl.pallas_call(
        functools.partial(_attn_sample_kernel, past=past, lam_init=lam_init),
        grid=(b, past // tk),
        in_specs=[
            smem, smem,
            pl.BlockSpec((None, nq, d), lambda b, k: (b, 0, 0)),
            pl.BlockSpec((None, tk * nh * 2, dh), lambda b, k: (b, k, 0)),
            pl.BlockSpec((None, tk * nh, dh), lambda b, k: (b, k, 0)),
            pl.BlockSpec((None, tk * nh, dh), lambda b, k: (b, k, 1)),
            pl.BlockSpec((None, nq, d), lambda b, k: (b, 0, 0)),
            pl.BlockSpec((None, nq, d), lambda b, k: (b, 0, 0)),
            pl.BlockSpec((1, hw), lambda b, k: (0, 0)),
        ],
        out_specs=pl.BlockSpec((None, nq, d), lambda b, k: (b, 0, 0)),
        out_shape=jax.ShapeDtypeStruct(q.shape, BF16),
        scratch_shapes=[
            pltpu.VMEM((ATTN_HEADS, 2 * nq, 128), F32),
            pltpu.VMEM((ATTN_HEADS, 2 * nq, 128), F32),
            pltpu.VMEM((ATTN_HEADS, 2 * nq, hw), F32),
        ],
        compiler_params=_params("parallel", "arbitrary"),
        name="attn_sample",
    )(slopes, lam, q, k_cache, v_cache, v_cache, k_new, v_new, subln.reshape(1, hw))


def _cumsum_rows(tri, x):
    hi = x.astype(BF16)
    r1 = x - hi.astype(F32)
    mid = r1.astype(BF16)
    lo = (r1 - mid.astype(F32)).astype(BF16)
    n = x.shape[1]
    y = jnp.dot(tri, jnp.concatenate([hi, mid, lo], axis=1), preferred_element_type=F32)
    return y[:, :n] + y[:, n:2 * n] + y[:, 2 * n:]


def _hgrn_kernel(q_ref, k_ref, i_ref, g_ref, gn_ref, s0_ref, o_ref, sout_ref,
                 s_ref, qb_ref, kb_ref, *, chunk, sub):
    tb = pl.program_id(2)
    hb, tbs, dk = q_ref.shape
    nch = tbs // chunk
    nsub = chunk // sub

    @pl.when(tb == 0)
    def _():
        s_ref[...] = s0_ref[...]
        qb_ref[...] = jnp.zeros(qb_ref.shape, BF16)
        kb_ref[...] = jnp.zeros(kb_ref.shape, BF16)

    r_io = lax.broadcasted_iota(jnp.int32, (chunk, chunk), 0)
    c_io = lax.broadcasted_iota(jnp.int32, (chunk, chunk), 1)
    causal = c_io <= r_io
    tri = causal.astype(BF16)

    for h in range(hb):
        parts = []
        for c in range(nch):
            t0, slot = c * chunk, h * nch + c
            kk = k_ref[h, t0:t0 + chunk, :]
            qs = q_ref[h, t0:t0 + chunk, :]
            v = i_ref[h, t0:t0 + chunk, :].astype(BF16)
            b = _cumsum_rows(tri, jnp.log(1.0 - kk))
            rows = [slice(sub * j, sub * (j + 1)) for j in range(nsub)]
            ends = [b[sub * (j + 1) - 1:sub * (j + 1), :] for j in range(nsub)]
            b_last = ends[-1]
            k_own = [kk[rows[j]] * jnp.exp(ends[j] - b[rows[j]]) for j in range(nsub)]
            q_end, k_end = [], []
            for i in range(nsub):
                lanes = slice(dk * i, dk * (i + 1))
                ref = ends[i - 1] if i > 0 else jnp.zeros((1, dk), F32)
                q_rel = qs[rows[i]] * jnp.exp(b[rows[i]] - ref)
                qb_ref[slot, rows[i], lanes] = q_rel.astype(BF16)
                q_end.append(q_rel * jnp.exp(ref))
                for j in range(i):
                    k_rel = k_own[j] if j == i - 1 else k_own[j] * jnp.exp(ref - ends[j])
                    kb_ref[slot, rows[j], lanes] = k_rel.astype(BF16)
                kb_ref[slot, rows[i], lanes] = (
                    kk[rows[i]] * jnp.exp(jnp.minimum(ref - b[rows[i]], EXP_CLAMP))).astype(BF16)
                k_end.append(k_own[i] if i == nsub - 1 else k_own[i] * jnp.exp(b_last - ends[i]))
            a = lax.dot_general(qb_ref[slot], kb_ref[slot], _NT, preferred_element_type=F32)
            a = jnp.where(causal, a, 0.0).astype(BF16)
            q_end = jnp.concatenate(q_end, axis=0).astype(BF16)
            k_end = jnp.concatenate(k_end, axis=0).astype(BF16)
            upd = lax.dot_general(k_end, v, _TN, preferred_element_type=F32)
            decay = jnp.transpose(jnp.broadcast_to(jnp.exp(b_last), (REC_DV, dk)))
            parts.append((jnp.concatenate([a, q_end], axis=1), v, decay, upd))
        state = s_ref[h]
        for c in range(nch):
            t0 = c * chunk
            a_q, v, decay, upd = parts[c]
            o = jnp.dot(a_q, jnp.concatenate([v, state.astype(BF16)], axis=0),
                        preferred_element_type=F32)
            state = decay * state + upd
            y = _rms(o, gn_ref[...]) * g_ref[h, t0:t0 + chunk, :]
            o_ref[t0:t0 + chunk, h * REC_DV:(h + 1) * REC_DV] = y.astype(o_ref.dtype)
        s_ref[h] = state
    sout_ref[...] = s_ref[...]


def hgrn(qfig, s0, out_norm, hb, tbs, chunk, sub):
    _, nh, b, t, dk = qfig.shape
    dv = REC_DV
    assert t % tbs == 0 and tbs % chunk == 0 and chunk % sub == 0 and sub % 16 == 0 and nh % hb == 0
    nsub = chunk // sub
    nslot = hb * (tbs // chunk)

    def kind(k):
        return pl.BlockSpec((None, hb, None, tbs, dk), lambda b, h, i: (k, h, b, i, 0))

    return pl.pallas_call(
        functools.partial(_hgrn_kernel, chunk=chunk, sub=sub),
        grid=(b, nh // hb, t // tbs),
        in_specs=[
            kind(0), kind(1), kind(2), kind(3),
            pl.BlockSpec((1, dv), lambda b, h, i: (0, 0)),
            pl.BlockSpec((None, hb, dk, dv), lambda b, h, i: (b, h, 0, 0)),
        ],
        out_specs=[
            pl.BlockSpec((None, tbs, hb * dv), lambda b, h, i: (b, i, h)),
            pl.BlockSpec((None, hb, dk, dv), lambda b, h, i: (b, h, 0, 0)),
        ],
        out_shape=[
            jax.ShapeDtypeStruct((b, t, nh * dv), BF16),
            jax.ShapeDtypeStruct((b, nh, dk, dv), F32),
        ],
        scratch_shapes=[
            pltpu.VMEM((hb, dk, dv), F32),
            pltpu.VMEM((nslot, chunk, nsub * dk), BF16),
            pltpu.VMEM((nslot, chunk, nsub * dk), BF16),
        ],
        compiler_params=_params("parallel", "parallel", "arbitrary"),
        name="hgrn",
    )(qfig, qfig, qfig, qfig, out_norm.reshape(1, dv), s0)


def _pick(n, candidates):
    for c in candidates:
        if n % c == 0:
            return c
    return n


def _trunk(x3, is_prompt, cache_k, cache_v, state_hgrn, state_conv, p):
    nseq, L, d = x3.shape
    m = nseq * L
    dff = p["ffn_w_down"].shape[1]
    tm = _pick(m, (1024, 512, 256))
    tf = _pick(dff, (512, 256, 128))
    if is_prompt:
        ffn_bs, ffn_lt = 1, _pick(L, (512, 256, 128))
    else:
        ffn_bs, ffn_lt = nseq, L
    x = x3.reshape(m, d)

    wqkv = p["attn_w_qkv"][0]
    nqk = ATTN_HEADS * 2 * ATTN_HEAD_DIM
    g0 = p["mixer_norm"][0]
    tmw = _pick(m, (512, 256))
    (q,) = norm_matmul(x, g0, wqkv, 0, nqk, ((BF16, "rows"),), tmw, nqk,
                       out_scale=ATTN_SCALE * LOG2E)
    k32, k16 = norm_matmul(x, g0, wqkv, nqk, nqk, ((F32, "interleaved"), (BF16, "rows")), tmw, nqk)
    v32, v16 = norm_matmul(x, g0, wqkv, 2 * nqk, nqk, ((F32, "rows"), (BF16, "rows")), tmw, nqk)
    sh = (nseq, L, nqk)
    lam_init = 0.8 - 0.6 * math.exp(-0.3 * 0)
    if is_prompt:
        o = attn_prompt(q.reshape(sh), k16.reshape(sh), v16.reshape(sh), p["slopes"], p["lam"],
                        p["attn_subln"][0], lam_init, tq=_pick(L, (512, 256, 128, 64)))
    else:
        past = cache_k.shape[2]
        o = attn_sample(q.reshape(sh), k16.reshape(sh), v16.reshape(sh),
                        cache_k[0].reshape(nseq, past * ATTN_HEADS * 2, ATTN_HEAD_DIM),
                        cache_v[0].reshape(nseq, past * ATTN_HEADS, ATTN_V_DIM),
                        p["slopes"], p["lam"], p["attn_subln"][0], lam_init,
                        tk=_pick(past, (512, 256, 128)))
    x, xn = matmul_res(o.reshape(m, nqk), p["attn_w_o"][0], x, p["ffn_norm"][0], tmw)
    prev0 = jnp.zeros((nseq, CONV_W - 1, dff), F32) if is_prompt else state_conv[0]
    x3b, cs0 = conv_ffn(x.reshape(nseq, L, d), xn.reshape(nseq, L, d), prev0, p["ffn_w_up"][0],
                        p["ffn_conv_w"][0], p["ffn_conv_b"][0], p["ffn_w_down"][0], None,
                        ffn_bs, ffn_lt, tf)
    x = x3b.reshape(m, d)

    wq = p["rec_w_qfig"][0]
    (qfig,) = norm_matmul(x, p["mixer_norm"][1], wq, 0, wq.shape[1], ((F32, "groups"),), tm, 512,
                          hgrn_lb=p["lb"])
    if is_prompt:
        s0 = jnp.zeros((nseq, REC_HEADS, REC_DK, REC_DV), F32)
        chunk = _pick(L, (128, 64, 32, 16))
        hb, tbs, sub = 2, _pick(L, (512, 256, 128, 64, 32, 16)), 16
    else:
        s0 = state_hgrn[0]
        hb, chunk, tbs, sub = REC_HEADS, L, L, 16
    o, s_fin = hgrn(qfig.reshape(4, REC_HEADS, nseq, L, REC_DK), s0, p["rec_out_norm"][0],
                    hb, tbs, chunk, sub)
    x, xn = matmul_res(o.reshape(m, d), p["rec_w_o"][0], x, p["ffn_norm"][1], tmw)
    prev1 = jnp.zeros((nseq, CONV_W - 1, dff), F32) if is_prompt else state_conv[1]
    y, cs1 = conv_ffn(x.reshape(nseq, L, d), xn.reshape(nseq, L, d), prev1, p["ffn_w_up"][1],
                      p["ffn_conv_w"][1], p["ffn_conv_b"][1], p["ffn_w_down"][1], p["final_norm"],
                      ffn_bs, ffn_lt, tf)
    kshape = (1, nseq, L, ATTN_HEADS, 2, ATTN_HEAD_DIM)
    vshape = (1, nseq, L, ATTN_HEADS, ATTN_V_DIM)
    return (y, k32.reshape(kshape), v32.reshape(vshape), s_fin[None], jnp.stack([cs0, cs1]))


def kernel(x_prompt, x_sample, cache_k, cache_v, state_hgrn, state_conv, mixer_norm, ffn_norm,
           attn_w_qkv, attn_lambda_q1, attn_lambda_k1, attn_lambda_q2, attn_lambda_k2, attn_subln,
           attn_w_o, rec_w_qfig, rec_lower_bounds, rec_out_norm, rec_w_o, ffn_w_up, ffn_conv_w,
           ffn_conv_b, ffn_w_down, final_norm):
    lam_init = 0.8 - 0.6 * math.exp(-0.3 * 0)
    lam = (jnp.exp(jnp.sum(attn_lambda_q1[0] * attn_lambda_k1[0]))
           - jnp.exp(jnp.sum(attn_lambda_q2[0] * attn_lambda_k2[0])) + lam_init).reshape(1)
    slopes = jnp.exp2(-8.0 * jnp.arange(1, ATTN_HEADS + 1, dtype=F32) / ATTN_HEADS)
    lbs = jax.nn.softmax(rec_lower_bounds.astype(F32), axis=0)
    lb = (jnp.cumsum(lbs, axis=0) - lbs[0])[1]
    p = dict(
        mixer_norm=mixer_norm, ffn_norm=ffn_norm, attn_subln=attn_subln, rec_out_norm=rec_out_norm,
        ffn_conv_w=ffn_conv_w, ffn_conv_b=ffn_conv_b, final_norm=final_norm,
        lam=lam, slopes=slopes, lb=lb,
        attn_w_qkv=attn_w_qkv.astype(BF16), attn_w_o=attn_w_o.astype(BF16),
        rec_w_qfig=rec_w_qfig.astype(BF16), rec_w_o=rec_w_o.astype(BF16),
        ffn_w_up=ffn_w_up.astype(BF16), ffn_w_down=ffn_w_down.astype(BF16),
    )
    yp, kp, vp, sp, cp = _trunk(x_prompt, True, None, None, None, None, p)
    ys, ks, vs, ss, cs = _trunk(x_sample, False, cache_k, cache_v, state_hgrn, state_conv, p)
    return (yp, ys, kp, vp, ks, vs, sp, ss, cp, cs)
```

```python
import functools
import math

import jax
import jax.numpy as jnp
from jax import lax
from jax.experimental import pallas as pl
from jax.experimental.pallas import tpu as pltpu

F32 = jnp.float32
BF16 = jnp.bfloat16

CHUNK = 64
EPS = 1e-6
ATTN_HEADS = 8
ATTN_HEAD_DIM = 128
ATTN_V_DIM = 2 * ATTN_HEAD_DIM
ATTN_SCALE = ATTN_HEAD_DIM ** -0.5
LOG2E = math.log2(math.e)
REC_HEADS = 16
REC_DK = 128
REC_DV = 128
CONV_W = 3
NEG_BIG = -1e30
EXP_CLAMP = 80.0

V7X_VMEM_LIMIT = 56 * 1024 * 1024

_NT = (((1,), (1,)), ((), ()))
_TN = (((0,), (0,)), ((), ()))


def _params(*sem):
    return pltpu.CompilerParams(dimension_semantics=sem, vmem_limit_bytes=V7X_VMEM_LIMIT)


def _rms(x, g):
    ms = jnp.mean(x * x, axis=-1, keepdims=True)
    return (x * lax.rsqrt(ms + EPS)) * g


def _silu(x):
    return x / (1.0 + jnp.exp(-x))


LANES = 128
NORM_MATMUL_CHUNK = 256
SAMPLE_EXPAND = 4


def _norm_matmul_kernel(x_ref, g_ref, w_ref, *rest, layouts, out_scale, gate_tiles):
    if gate_tiles:
        lb_ref, rest = rest[0], rest[1:]
    outs, xn_ref = rest[:len(layouts)], rest[len(layouts)]
    j = pl.program_id(1)

    @pl.when(j == 0)
    def _():
        xn_ref[...] = _rms(x_ref[...], g_ref[...]).astype(BF16)

    tm, tn = xn_ref.shape[0], w_ref.shape[1]
    ngrp = tn // LANES
    cw = min(tn, NORM_MATMUL_CHUNK)

    def run(act):
        for c0 in range(0, tn, cw):
            val = jnp.dot(xn_ref[...], w_ref[:, c0:c0 + cw], preferred_element_type=F32)
            if out_scale is not None:
                val = val * out_scale
            val = act(val, c0)
            for o, layout in zip(outs, layouts):
                if layout == "rows":
                    o[:, c0:c0 + cw] = val.astype(o.dtype)
                    continue
                for gi in range(cw // LANES):
                    piece = val[:, gi * LANES:(gi + 1) * LANES].astype(o.dtype)
                    if layout == "groups":
                        o[c0 // LANES + gi] = piece
                    else:
                        o[pl.ds(c0 // LANES + gi, tm, stride=ngrp), :] = piece

    if not gate_tiles:
        run(lambda val, c0: val)
    else:
        kind = j // gate_tiles

        @pl.when((kind == 0) | (kind == 3))
        def _():
            run(lambda val, c0: _silu(val))

        @pl.when(kind == 1)
        def _():
            run(lambda val, c0: (1.0 - lb_ref[:, c0:c0 + cw]) / (1.0 + jnp.exp(val)))

        @pl.when(kind == 2)
        def _():
            run(lambda val, c0: val)


def norm_matmul(x, g, w, col0, ncols, outs, tm, tn, out_scale=None, hgrn_lb=None):
    m, d = x.shape
    assert m % tm == 0 and ncols % tn == 0 and col0 % tn == 0 and tn % LANES == 0
    joff = col0 // tn
    ngrp = tn // LANES
    extra_in, extra_specs, gate_tiles = [], [], 0
    if hgrn_lb is not None:
        quarter = ncols // 4
        assert quarter % tn == 0 and col0 == 0
        gate_tiles = quarter // tn
        zeros = jnp.zeros((quarter,), F32)
        extra_in = [jnp.concatenate([zeros, hgrn_lb, zeros, zeros]).reshape(1, ncols)]
        extra_specs = [pl.BlockSpec((1, tn), lambda i, j: (0, j))]
    specs, shapes = [], []
    for dt, layout in outs:
        if layout == "rows":
            specs.append(pl.BlockSpec((tm, tn), lambda i, j: (i, j)))
            shapes.append(jax.ShapeDtypeStruct((m, ncols), dt))
        elif layout == "groups":
            specs.append(pl.BlockSpec((ngrp, tm, LANES), lambda i, j: (j, i, 0)))
            shapes.append(jax.ShapeDtypeStruct((ncols // LANES, m, LANES), dt))
        else:
            assert layout == "interleaved" and tn == ncols and dt == F32
            specs.append(pl.BlockSpec((tm * ngrp, LANES), lambda i, j: (i, 0)))
            shapes.append(jax.ShapeDtypeStruct((m * ngrp, LANES), dt))
    return pl.pallas_call(
        functools.partial(_norm_matmul_kernel, layouts=tuple(l for _, l in outs),
                          out_scale=out_scale, gate_tiles=gate_tiles),
        grid=(m // tm, ncols // tn),
        in_specs=[
            pl.BlockSpec((tm, d), lambda i, j: (i, 0)),
            pl.BlockSpec((1, d), lambda i, j: (0, 0)),
            pl.BlockSpec((d, tn), lambda i, j: (0, j + joff)),
        ] + extra_specs,
        out_specs=specs,
        out_shape=shapes,
        scratch_shapes=[pltpu.VMEM((tm, d), BF16)],
        compiler_params=_params("parallel", "arbitrary"),
        name="norm_matmul",
    )(x, g.reshape(1, d), w, *extra_in)


def _matmul_res_kernel(a_ref, w_ref, r_ref, g_ref, o_ref, on_ref):
    out = r_ref[...] + jnp.dot(a_ref[...], w_ref[...], preferred_element_type=F32)
    o_ref[...] = out
    on_ref[...] = _rms(out, g_ref[...]).astype(on_ref.dtype)


def matmul_res(a, w, res, g, tm):
    m, k = a.shape
    n = w.shape[1]
    assert m % tm == 0
    return pl.pallas_call(
        _matmul_res_kernel,
        grid=(m // tm,),
        in_specs=[
            pl.BlockSpec((tm, k), lambda i: (i, 0)),
            pl.BlockSpec((k, n), lambda i: (0, 0)),
            pl.BlockSpec((tm, n), lambda i: (i, 0)),
            pl.BlockSpec((1, n), lambda i: (0, 0)),
        ],
        out_specs=[pl.BlockSpec((tm, n), lambda i: (i, 0)), pl.BlockSpec((tm, n), lambda i: (i, 0))],
        out_shape=[jax.ShapeDtypeStruct((m, n), F32), jax.ShapeDtypeStruct((m, n), BF16)],
        compiler_params=_params("parallel"),
        name="matmul_res",
    )(a, w, res, g.reshape(1, n))


def _ffn_kernel(x_ref, xn_ref, prev_ref, wg_ref, wv_ref, cw_ref, cb_ref, wd_ref, *rest,
                final_norm, nsplit):
    if final_norm:
        fg_ref, rest = rest[0], rest[1:]
    y_ref, cs_ref, ext_ref, carry_ref = rest
    l = pl.program_id(1)
    j = pl.program_id(2)
    nj = pl.num_programs(2)
    bs, lt, d = x_ref.shape
    tf = wg_ref.shape[1]
    halo = CONV_W - 1

    @pl.when(j == 0)
    def _():
        y_ref[...] = x_ref[...]

    @pl.when(l == 0)
    def _():
        ext_ref[0, :, 8 - halo:8, :] = prev_ref[...]

    @pl.when(l > 0)
    def _():
        ext_ref[0, :, 8 - halo:8, :] = carry_ref[j]

    lp = lt // nsplit
    rp = bs * lp
    last = None
    for s in range(nsplit):
        xn = xn_ref[:, s * lp:(s + 1) * lp, :].reshape(rp, d)
        gate = jnp.dot(xn, wg_ref[...], preferred_element_type=F32).reshape(bs, lp, tf)
        val = jnp.dot(xn, wv_ref[...], preferred_element_type=F32).reshape(bs, lp, tf)
        if s > 0:
            ext_ref[s, :, 8 - halo:8, :] = last
        ext_ref[s, :, 8:8 + lp, :] = gate
        last = gate[:, lp - halo:lp, :]
        c = cb_ref[...].reshape(1, 1, tf)
        for t in range(CONV_W):
            off = 8 - halo + t
            c = c + ext_ref[s, :, off:off + lp, :] * cw_ref[t:t + 1, :].reshape(1, 1, tf)
        act = (_silu(c) * val).reshape(rp, tf).astype(BF16)
        upd = jnp.dot(act, wd_ref[...], preferred_element_type=F32)
        y_ref[:, s * lp:(s + 1) * lp, :] += upd.reshape(bs, lp, d)
    carry_ref[j] = last
    cs_ref[...] = last

    if final_norm:
        @pl.when(j == nj - 1)
        def _():
            y_ref[...] = _rms(y_ref[...], fg_ref[...].reshape(1, 1, d))


def conv_ffn(x3, xn3, prev, w_up, conv_w, conv_b, w_down, final_g, bs, lt, tf):
    nseq, L, d = x3.shape
    dff = w_down.shape[0]
    assert nseq % bs == 0 and L % lt == 0 and dff % tf == 0 and lt % 8 == 0
    nj = dff // tf
    halo = CONV_W - 1
    final_norm = final_g is not None
    fg_in = [final_g.reshape(1, d)] if final_norm else []
    fg_spec = [pl.BlockSpec((1, d), lambda s, l, j: (0, 0))] if final_norm else []
    nsplit = 2 if (bs == 1 and lt % 256 == 0) else 1
    y, tails = pl.pallas_call(
        functools.partial(_ffn_kernel, final_norm=final_norm, nsplit=nsplit),
        grid=(nseq // bs, L // lt, nj),
        in_specs=[
            pl.BlockSpec((bs, lt, d), lambda s, l, j: (s, l, 0)),
            pl.BlockSpec((bs, lt, d), lambda s, l, j: (s, l, 0)),
            pl.BlockSpec((bs, halo, tf), lambda s, l, j: (s, 0, j)),
            pl.BlockSpec((d, tf), lambda s, l, j: (0, j)),
            pl.BlockSpec((d, tf), lambda s, l, j: (0, j + nj)),
            pl.BlockSpec((CONV_W, tf), lambda s, l, j: (0, j)),
            pl.BlockSpec((1, tf), lambda s, l, j: (0, j)),
            pl.BlockSpec((tf, d), lambda s, l, j: (j, 0)),
        ] + fg_spec,
        out_specs=[
            pl.BlockSpec((bs, lt, d), lambda s, l, j: (s, l, 0)),
            pl.BlockSpec((bs, None, halo, tf), lambda s, l, j: (s, l, 0, j)),
        ],
        out_shape=[
            jax.ShapeDtypeStruct((nseq, L, d), F32),
            jax.ShapeDtypeStruct((nseq, L // lt, halo, dff), F32),
        ],
        scratch_shapes=[
            pltpu.VMEM((nsplit, bs, lt // nsplit + 8, tf), F32),
            pltpu.VMEM((nj, bs, halo, tf), F32),
        ],
        compiler_params=_params("parallel", "arbitrary", "arbitrary"),
        name="conv_ffn",
    )(x3, xn3, prev, w_up, w_up, conv_w, conv_b.reshape(1, dff), w_down, *fg_in)
    return y, tails[:, -1]


def _softmax_step(s, m, l, acc, vb):
    m_new = jnp.maximum(m, jnp.max(s, axis=-1, keepdims=True))
    alpha = jnp.exp2(m - m_new)
    p = jnp.exp2(s - m_new)
    l_new = alpha * l + jnp.sum(p, axis=-1, keepdims=True)
    acc_new = alpha * acc + jnp.dot(p.astype(BF16), vb, preferred_element_type=F32)
    return m_new, l_new, acc_new


def _subln(o, g, lam_init):
    return _rms(o, g) * (1.0 - lam_init)


def _attn_prompt_kernel(slopes_ref, lam_ref, q_ref, k_ref, v_ref, g_ref, o_ref,
                        vt_ref, own_ref, p_ref, acc_ref, *, lam_init):
    h = pl.program_id(1)
    qi = pl.program_id(2)
    t = q_ref.shape[0]
    dh = ATTN_HEAD_DIM
    slope2 = slopes_ref[h] * LOG2E
    lam = lam_ref[0]
    q0 = qi * t

    @pl.when(qi == 0)
    def _():
        vt_ref[...] = v_ref[...].T
        k_io = lax.broadcasted_iota(jnp.int32, (t, t), 0)
        q_io = lax.broadcasted_iota(jnp.int32, (t, t), 1)
        own_ref[...] = jnp.where((k_io // CHUNK) <= (q_io // CHUNK),
                                 slope2 * (q_io - jnp.abs(q_io - k_io)).astype(F32), NEG_BIG)

    q = q_ref[...]

    def probs(c, kb, bias, m, l):
        s = lax.dot_general(kb[:, c * dh:(c + 1) * dh], q[:, c * dh:(c + 1) * dh], _NT,
                            preferred_element_type=F32) + bias
        m_new = jnp.maximum(m, jnp.max(s, axis=0, keepdims=True))
        alpha = jnp.exp2(m - m_new)
        p = jnp.exp2(s - m_new)
        return m_new, alpha * l + jnp.sum(p, axis=0, keepdims=True), alpha, p.astype(BF16)

    def add_values(c, k0, alpha, p):
        acc_ref[c] = alpha * acc_ref[c] + jnp.dot(vt_ref[:, pl.ds(k0, t)], p,
                                                  preferred_element_type=F32)

    def tile(k0, prev0, bias, state):
        (m0, l0), (m1, l1, alpha1) = state
        kb = k_ref[pl.ds(k0, t), :]
        m0, l0, alpha0, p0 = probs(0, kb, bias, m0, l0)
        if prev0 is not None:
            add_values(1, prev0, alpha1, p_ref[...])
        m1, l1, alpha1, p1 = probs(1, kb, bias, m1, l1)
        p_ref[...] = p1
        add_values(0, k0, alpha0, p0)
        return (m0, l0), (m1, l1, alpha1)

    neg, zero = jnp.full((1, t), NEG_BIG, F32), jnp.zeros((1, t), F32)
    acc_ref[...] = jnp.zeros(acc_ref.shape, F32)
    own0 = pl.multiple_of(q0, t)
    state = tile(own0, None, own_ref[...], ((neg, zero), (neg, zero, zero)))

    key = lax.broadcasted_iota(jnp.int32, (t, LANES), 0)

    def past(kt, state):
        k0 = pl.multiple_of(kt * t, t)
        prev0 = pl.multiple_of(jnp.where(kt == 0, q0, k0 - t), t)
        bias = slope2 * (key + (k0 - q0)).astype(F32)
        return tile(k0, prev0, jnp.concatenate([bias] * (t // LANES), axis=1), state)

    (_, l0), (_, l1, alpha1) = lax.fori_loop(0, qi, past, state)
    add_values(1, pl.multiple_of(jnp.where(qi == 0, q0, q0 - t), t), alpha1, p_ref[...])
    a0, a1 = acc_ref[0], acc_ref[1]
    o = (a0 / l0 - lam * (a1 / l1)).T
    o_ref[...] = _subln(o, g_ref[...], lam_init).astype(o_ref.dtype)


def attn_prompt(q, k, v, slopes, lam, subln, lam_init, tq):
    b, t, _ = q.shape
    hw = 2 * ATTN_HEAD_DIM
    assert t % tq == 0 and tq % CHUNK == 0
    smem = pl.BlockSpec(memory_space=pltpu.SMEM)
    return pl.pallas_call(
        functools.partial(_attn_prompt_kernel, lam_init=lam_init),
        grid=(b, ATTN_HEADS, t // tq),
        in_specs=[
            smem, smem,
            pl.BlockSpec((None, tq, hw), lambda b, h, i: (b, i, h)),
            pl.BlockSpec((None, t, hw), lambda b, h, i: (b, 0, h)),
            pl.BlockSpec((None, t, hw), lambda b, h, i: (b, 0, h)),
            pl.BlockSpec((1, hw), lambda b, h, i: (0, 0)),
        ],
        out_specs=pl.BlockSpec((None, tq, hw), lambda b, h, i: (b, i, h)),
        out_shape=jax.ShapeDtypeStruct(q.shape, BF16),
        scratch_shapes=[
            pltpu.VMEM((hw, t), BF16),
            pltpu.VMEM((tq, tq), F32),
            pltpu.VMEM((tq, tq), BF16),
            pltpu.VMEM((2, hw, tq), F32),
        ],
        compiler_params=_params("parallel", "parallel", "arbitrary"),
        name="attn_prompt",
    )(slopes, lam, q, k, v, subln.reshape(1, hw))


def _attn_sample_kernel(slopes_ref, lam_ref, q_ref, kc_ref, vlo_ref, vhi_ref, kn_ref, vn_ref, g_ref,
                        o_ref, m_ref, l_ref, acc_ref, bias_ref, *, past, lam_init):
    kt = pl.program_id(1)
    nkt = pl.num_programs(1)
    nq = q_ref.shape[0]
    nh, ex = ATTN_HEADS, SAMPLE_EXPAND
    ng = nh // ex
    tk = vlo_ref.shape[0] // nh
    n = ex * tk
    rows = ex * nq
    dh = ATTN_HEAD_DIM
    hw = 2 * dh
    lam = lam_ref[0]

    def q_stack(g, c):
        return jnp.concatenate([q_ref[:, (g + ng * e) * hw + c * dh:(g + ng * e) * hw + (c + 1) * dh]
                                for e in range(ex)], axis=0)

    def scores(g, keys):
        return jnp.concatenate([lax.dot_general(q_stack(g, c), keys[c], _NT,
                                                preferred_element_type=F32) for c in range(2)], axis=0)

    def slope_rows(g):
        e_io = lax.broadcasted_iota(jnp.int32, (2 * rows, 1), 0) % rows // nq
        out = jnp.zeros((2 * rows, 1), F32)
        for e in range(ex):
            out = jnp.where(e_io == e, slopes_ref[g + ng * e] * LOG2E, out)
        return out

    def update(g, s, vb):
        m, l, acc = _softmax_step(s, m_ref[g][:, :1], l_ref[g][:, :1], acc_ref[g], vb)
        m_ref[g] = jnp.broadcast_to(m, m_ref.shape[1:])
        l_ref[g] = jnp.broadcast_to(l, l_ref.shape[1:])
        acc_ref[g] = acc

    @pl.when(kt == 0)
    def _():
        m_ref[...] = jnp.full(m_ref.shape, NEG_BIG, F32)
        l_ref[...] = jnp.zeros(l_ref.shape, F32)
        acc_ref[...] = jnp.zeros(acc_ref.shape, F32)
        e_row = lax.broadcasted_iota(jnp.int32, (2 * rows, n), 0) % rows // nq
        c_io = lax.broadcasted_iota(jnp.int32, (2 * rows, n), 1)
        for g in range(ng):
            bias_ref[g] = jnp.where(e_row == c_io % ex,
                                    slope_rows(g) * (c_io // ex - past).astype(F32), NEG_BIG)

    for g in range(ng):
        keys = [kc_ref[pl.ds(2 * g + c, n, stride=2 * ng), :].astype(BF16) for c in range(2)]
        vals = jnp.concatenate([vlo_ref[pl.ds(g, n, stride=ng), :], vhi_ref[pl.ds(g, n, stride=ng), :]],
                               axis=1).astype(BF16)
        update(g, scores(g, keys) + (bias_ref[g] + slope_rows(g) * (kt * tk).astype(F32)), vals)

    @pl.when(kt == nkt - 1)
    def _():
        r_io = lax.broadcasted_iota(jnp.int32, (2 * rows, rows), 0) % rows
        c_io = lax.broadcasted_iota(jnp.int32, (2 * rows, rows), 1)
        tok, key = r_io % nq, c_io % nq
        rel_n = (tok - jnp.abs(tok - key)).astype(F32)
        allowed = (r_io // nq == c_io // nq) & (((key + past) // CHUNK) <= ((tok + past) // CHUNK))
        for g in range(ng):
            heads = [g + ng * e for e in range(ex)]
            keys = [jnp.concatenate([kn_ref[:, h * hw + c * dh:h * hw + (c + 1) * dh] for h in heads],
                                    axis=0) for c in range(2)]
            vals = jnp.concatenate([vn_ref[:, h * hw:(h + 1) * hw] for h in heads], axis=0)
            update(g, jnp.where(allowed, scores(g, keys) + slope_rows(g) * rel_n, NEG_BIG), vals)
            on = acc_ref[g] / l_ref[g][:, :1]
            for e, h in enumerate(heads):
                o = on[e * nq:(e + 1) * nq] - lam * on[rows + e * nq:rows + (e + 1) * nq]
                o_ref[:, h * hw:(h + 1) * hw] = _subln(o, g_ref[...], lam_init).astype(o_ref.dtype)


def attn_sample(q, k_new, v_new, k_cache, v_cache, slopes, lam, subln, lam_init, tk):
    b, nq, d = q.shape
    nh, dh = ATTN_HEADS, ATTN_HEAD_DIM
    past = v_cache.shape[1] // nh
    hw = 2 * dh
    ex = SAMPLE_EXPAND
    ng = nh // ex
    assert past % tk == 0 and past > 0 and dh == LANES and nh % ex == 0
    smem = pl.BlockSpec(memory_space=pltpu.SMEM)
    return pl.pallas_call(
        functools.partial(_attn_sample_kernel, past=past, lam_init=lam_init),
        grid=(b, past // tk),
        in_specs=[
            smem, smem,
            pl.BlockSpec((None, nq, d), lambda b, k: (b, 0, 0)),
            pl.BlockSpec((None, tk * nh * 2, dh), lambda b, k: (b, k, 0)),
            pl.BlockSpec((None, tk * nh, dh), lambda b, k: (b, k, 0)),
            pl.BlockSpec((None, tk * nh, dh), lambda b, k: (b, k, 1)),
            pl.BlockSpec((None, nq, d), lambda b, k: (b, 0, 0)),
            pl.BlockSpec((None, nq, d), lambda b, k: (b, 0, 0)),
            pl.BlockSpec((1, hw), lambda b, k: (0, 0)),
        ],
        out_specs=pl.BlockSpec((None, nq, d), lambda b, k: (b, 0, 0)),
        out_shape=jax.ShapeDtypeStruct(q.shape, BF16),
        scratch_shapes=[
            pltpu.VMEM((ng, 2 * ex * nq, LANES), F32),
            pltpu.VMEM((ng, 2 * ex * nq, LANES), F32),
            pltpu.VMEM((ng, 2 * ex * nq, hw), F32),
            pltpu.VMEM((ng, 2 * ex * nq, ex * tk), F32),
        ],
        compiler_params=_params("parallel", "arbitrary"),
        name="attn_sample",
    )(slopes, lam, q, k_cache, v_cache, v_cache, k_new, v_new, subln.reshape(1, hw))


def _cumsum_rows(tri, x):
    hi = x.astype(BF16)
    r1 = x - hi.astype(F32)
    mid = r1.astype(BF16)
    lo = (r1 - mid.astype(F32)).astype(BF16)
    n = x.shape[1]
    y = jnp.dot(tri, jnp.concatenate([hi, mid, lo], axis=1), preferred_element_type=F32)
    return y[:, :n] + y[:, n:2 * n] + y[:, 2 * n:]


def _hgrn_kernel(q_ref, k_ref, i_ref, g_ref, gn_ref, s0_ref, o_ref, sout_ref,
                 s_ref, qb_ref, kb_ref, *, chunk, sub):
    tb = pl.program_id(2)
    hb, tbs, dk = q_ref.shape
    nch = tbs // chunk
    nsub = chunk // sub

    @pl.when(tb == 0)
    def _():
        s_ref[...] = s0_ref[...]
        qb_ref[...] = jnp.zeros(qb_ref.shape, BF16)
        kb_ref[...] = jnp.zeros(kb_ref.shape, BF16)

    r_io = lax.broadcasted_iota(jnp.int32, (chunk, chunk), 0)
    c_io = lax.broadcasted_iota(jnp.int32, (chunk, chunk), 1)
    causal = c_io <= r_io
    tri = causal.astype(BF16)

    for h in range(hb):
        parts = []
        for c in range(nch):
            t0, slot = c * chunk, h * nch + c
            kk = k_ref[h, t0:t0 + chunk, :]
            qs = q_ref[h, t0:t0 + chunk, :]
            v = i_ref[h, t0:t0 + chunk, :].astype(BF16)
            b = _cumsum_rows(tri, jnp.log(1.0 - kk))
            rows = [slice(sub * j, sub * (j + 1)) for j in range(nsub)]
            ends = [b[sub * (j + 1) - 1:sub * (j + 1), :] for j in range(nsub)]
            b_last = ends[-1]
            k_own = [kk[rows[j]] * jnp.exp(ends[j] - b[rows[j]]) for j in range(nsub)]
            q_end, k_end = [], []
            for i in range(nsub):
                lanes = slice(dk * i, dk * (i + 1))
                ref = ends[i - 1] if i > 0 else jnp.zeros((1, dk), F32)
                q_rel = qs[rows[i]] * jnp.exp(b[rows[i]] - ref)
                qb_ref[slot, rows[i], lanes] = q_rel.astype(BF16)
                q_end.append(q_rel * jnp.exp(ref))
                for j in range(i):
                    k_rel = k_own[j] if j == i - 1 else k_own[j] * jnp.exp(ref - ends[j])
                    kb_ref[slot, rows[j], lanes] = k_rel.astype(BF16)
                kb_ref[slot, rows[i], lanes] = (
                    kk[rows[i]] * jnp.exp(jnp.minimum(ref - b[rows[i]], EXP_CLAMP))).astype(BF16)
                k_end.append(k_own[i] if i == nsub - 1 else k_own[i] * jnp.exp(b_last - ends[i]))
            a = lax.dot_general(qb_ref[slot], kb_ref[slot], _NT, preferred_element_type=F32)
            a = jnp.where(causal, a, 0.0).astype(BF16)
            q_end = jnp.concatenate(q_end, axis=0).astype(BF16)
            k_end = jnp.concatenate(k_end, axis=0).astype(BF16)
            upd = lax.dot_general(k_end, v, _TN, preferred_element_type=F32)
            decay = jnp.transpose(jnp.broadcast_to(jnp.exp(b_last), (REC_DV, dk)))
            parts.append((jnp.concatenate([a, q_end], axis=1), v, decay, upd))
        state = s_ref[h]
        for c in range(nch):
            t0 = c * chunk
            a_q, v, decay, upd = parts[c]
            o = jnp.dot(a_q, jnp.concatenate([v, state.astype(BF16)], axis=0),
                        preferred_element_type=F32)
            state = decay * state + upd
            y = _rms(o, gn_ref[...]) * g_ref[h, t0:t0 + chunk, :]
            o_ref[t0:t0 + chunk, h * REC_DV:(h + 1) * REC_DV] = y.astype(o_ref.dtype)
        s_ref[h] = state
    sout_ref[...] = s_ref[...]


def hgrn(qfig, s0, out_norm, hb, tbs, chunk, sub):
    _, nh, b, t, dk = qfig.shape
    dv = REC_DV
    assert t % tbs == 0 and tbs % chunk == 0 and chunk % sub == 0 and sub % 16 == 0 and nh % hb == 0
    nsub = chunk // sub
    nslot = hb * (tbs // chunk)

    def kind(k):
        return pl.BlockSpec((None, hb, None, tbs, dk), lambda b, h, i: (k, h, b, i, 0))

    return pl.pallas_call(
        functools.partial(_hgrn_kernel, chunk=chunk, sub=sub),
        grid=(b, nh // hb, t // tbs),
        in_specs=[
            kind(0), kind(1), kind(2), kind(3),
            pl.BlockSpec((1, dv), lambda b, h, i: (0, 0)),
            pl.BlockSpec((None, hb, dk, dv), lambda b, h, i: (b, h, 0, 0)),
        ],
        out_specs=[
            pl.BlockSpec((None, tbs, hb * dv), lambda b, h, i: (b, i, h)),
            pl.BlockSpec((None, hb, dk, dv), lambda b, h, i: (b, h, 0, 0)),
        ],
        out_shape=[
            jax.ShapeDtypeStruct((b, t, nh * dv), BF16),
            jax.ShapeDtypeStruct((b, nh, dk, dv), F32),
        ],
        scratch_shapes=[
            pltpu.VMEM((hb, dk, dv), F32),
            pltpu.VMEM((nslot, chunk, nsub * dk), BF16),
            pltpu.VMEM((nslot, chunk, nsub * dk), BF16),
        ],
        compiler_params=_params("parallel", "parallel", "arbitrary"),
        name="hgrn",
    )(qfig, qfig, qfig, qfig, out_norm.reshape(1, dv), s0)


def _pick(n, candidates):
    for c in candidates:
        if n % c == 0:
            return c
    return n


def _trunk(x3, is_prompt, cache_k, cache_v, state_hgrn, state_conv, p):
    nseq, L, d = x3.shape
    m = nseq * L
    dff = p["ffn_w_down"].shape[1]
    tm = _pick(m, (1024, 512, 256))
    tf = _pick(dff, (512, 256, 128))
    if is_prompt:
        ffn_bs, ffn_lt = 1, _pick(L, (512, 256, 128))
    else:
        ffn_bs, ffn_lt = nseq, L
    x = x3.reshape(m, d)

    wqkv = p["attn_w_qkv"][0]
    nqk = ATTN_HEADS * 2 * ATTN_HEAD_DIM
    g0 = p["mixer_norm"][0]
    tmw = _pick(m, (512, 256))
    (q,) = norm_matmul(x, g0, wqkv, 0, nqk, ((BF16, "rows"),), tmw, nqk,
                       out_scale=ATTN_SCALE * LOG2E)
    k32, k16 = norm_matmul(x, g0, wqkv, nqk, nqk, ((F32, "interleaved"), (BF16, "rows")), tmw, nqk)
    v32, v16 = norm_matmul(x, g0, wqkv, 2 * nqk, nqk, ((F32, "rows"), (BF16, "rows")), tmw, nqk)
    sh = (nseq, L, nqk)
    lam_init = 0.8 - 0.6 * math.exp(-0.3 * 0)
    if is_prompt:
        o = attn_prompt(q.reshape(sh), k16.reshape(sh), v16.reshape(sh), p["slopes"], p["lam"],
                        p["attn_subln"][0], lam_init, tq=_pick(L, (512, 256, 128, 64)))
    else:
        past = cache_k.shape[2]
        o = attn_sample(q.reshape(sh), k16.reshape(sh), v16.reshape(sh),
                        cache_k[0].reshape(nseq, past * ATTN_HEADS * 2, ATTN_HEAD_DIM),
                        cache_v[0].reshape(nseq, past * ATTN_HEADS, ATTN_V_DIM),
                        p["slopes"], p["lam"], p["attn_subln"][0], lam_init,
                        tk=_pick(past, (512, 256, 128)))
    x, xn = matmul_res(o.reshape(m, nqk), p["attn_w_o"][0], x, p["ffn_norm"][0], tmw)
    prev0 = jnp.zeros((nseq, CONV_W - 1, dff), F32) if is_prompt else state_conv[0]
    x3b, cs0 = conv_ffn(x.reshape(nseq, L, d), xn.reshape(nseq, L, d), prev0, p["ffn_w_up"][0],
                        p["ffn_conv_w"][0], p["ffn_conv_b"][0], p["ffn_w_down"][0], None,
                        ffn_bs, ffn_lt, tf)
    x = x3b.reshape(m, d)

    wq = p["rec_w_qfig"][0]
    (qfig,) = norm_matmul(x, p["mixer_norm"][1], wq, 0, wq.shape[1], ((F32, "groups"),), tm, 512,
                          hgrn_lb=p["lb"])
    if is_prompt:
        s0 = jnp.zeros((nseq, REC_HEADS, REC_DK, REC_DV), F32)
        chunk = _pick(L, (128, 64, 32, 16))
        hb, tbs, sub = 2, _pick(L, (1024, 512, 256, 128, 64, 32, 16)), 16
    else:
        s0 = state_hgrn[0]
        hb, chunk, tbs, sub = REC_HEADS, L, L, 16
    o, s_fin = hgrn(qfig.reshape(4, REC_HEADS, nseq, L, REC_DK), s0, p["rec_out_norm"][0],
                    hb, tbs, chunk, sub)
    x, xn = matmul_res(o.reshape(m, d), p["rec_w_o"][0], x, p["ffn_norm"][1], tmw)
    prev1 = jnp.zeros((nseq, CONV_W - 1, dff), F32) if is_prompt else state_conv[1]
    y, cs1 = conv_ffn(x.reshape(nseq, L, d), xn.reshape(nseq, L, d), prev1, p["ffn_w_up"][1],
                      p["ffn_conv_w"][1], p["ffn_conv_b"][1], p["ffn_w_down"][1], p["final_norm"],
                      ffn_bs, ffn_lt, tf)
    kshape = (1, nseq, L, ATTN_HEADS, 2, ATTN_HEAD_DIM)
    vshape = (1, nseq, L, ATTN_HEADS, ATTN_V_DIM)
    return (y, k32.reshape(kshape), v32.reshape(vshape), s_fin[None], jnp.stack([cs0, cs1]))


def kernel(x_prompt, x_sample, cache_k, cache_v, state_hgrn, state_conv, mixer_norm, ffn_norm,
           attn_w_qkv, attn_lambda_q1, attn_lambda_k1, attn_lambda_q2, attn_lambda_k2, attn_subln,
           attn_w_o, rec_w_qfig, rec_lower_bounds, rec_out_norm, rec_w_o, ffn_w_up, ffn_conv_w,
           ffn_conv_b, ffn_w_down, final_norm):
    lam_init = 0.8 - 0.6 * math.exp(-0.3 * 0)
    lam = (jnp.exp(jnp.sum(attn_lambda_q1[0] * attn_lambda_k1[0]))
           - jnp.exp(jnp.sum(attn_lambda_q2[0] * attn_lambda_k2[0])) + lam_init).reshape(1)
    slopes = jnp.exp2(-8.0 * jnp.arange(1, ATTN_HEADS + 1, dtype=F32) / ATTN_HEADS)
    lbs = jax.nn.softmax(rec_lower_bounds.astype(F32), axis=0)
    lb = (jnp.cumsum(lbs, axis=0) - lbs[0])[1]
    p = dict(
        mixer_norm=mixer_norm, ffn_norm=ffn_norm, attn_subln=attn_subln, rec_out_norm=rec_out_norm,
        ffn_conv_w=ffn_conv_w, ffn_conv_b=ffn_conv_b, final_norm=final_norm,
        lam=lam, slopes=slopes, lb=lb,
        attn_w_qkv=attn_w_qkv.astype(BF16), attn_w_o=attn_w_o.astype(BF16),
        rec_w_qfig=rec_w_qfig.astype(BF16), rec_w_o=rec_w_o.astype(BF16),
        ffn_w_up=ffn_w_up.astype(BF16), ffn_w_down=ffn_w_down.astype(BF16),
    )
    yp, kp, vp, sp, cp = _trunk(x_prompt, True, None, None, None, None, p)
    ys, ks, vs, ss, cs = _trunk(x_sample, False, cache_k, cache_v, state_hgrn, state_conv, p)
    return (yp, ys, kp, vp, ks, vs, sp, ss, cp, cs)
```

```python
import functools
import math

import jax
import jax.numpy as jnp
from jax import lax
from jax.experimental import pallas as pl
from jax.experimental.pallas import tpu as pltpu

F32 = jnp.float32
BF16 = jnp.bfloat16

CHUNK = 64
EPS = 1e-6
ATTN_HEADS = 8
ATTN_HEAD_DIM = 128
ATTN_V_DIM = 2 * ATTN_HEAD_DIM
ATTN_SCALE = ATTN_HEAD_DIM ** -0.5
LOG2E = math.log2(math.e)
REC_HEADS = 16
REC_DK = 128
REC_DV = 128
CONV_W = 3
NEG_BIG = -1e30
EXP_CLAMP = 80.0

V7X_VMEM_LIMIT = 56 * 1024 * 1024

_NT = (((1,), (1,)), ((), ()))
_TN = (((0,), (0,)), ((), ()))


def _params(*sem):
    return pltpu.CompilerParams(dimension_semantics=sem, vmem_limit_bytes=V7X_VMEM_LIMIT)


def _rms(x, g):
    ms = jnp.mean(x * x, axis=-1, keepdims=True)
    return (x * lax.rsqrt(ms + EPS)) * g


def _silu(x):
    return x / (1.0 + jnp.exp(-x))


LANES = 128
NORM_MATMUL_CHUNK = 256
SAMPLE_EXPAND = 4


def _norm_matmul_kernel(x_ref, *rest, layouts, out_scale, gate_tiles, prenorm):
    if not prenorm:
        g_ref, rest = rest[0], rest[1:]
    w_ref, rest = rest[0], rest[1:]
    if gate_tiles:
        lb_ref, rest = rest[0], rest[1:]
    outs = rest[:len(layouts)]
    xn_ref = x_ref if prenorm else rest[len(layouts)]
    j = pl.program_id(1)

    if not prenorm:
        @pl.when(j == 0)
        def _():
            xn_ref[...] = _rms(x_ref[...], g_ref[...]).astype(BF16)
            for o, layout in zip(outs, layouts):
                if layout == "xn":
                    o[...] = xn_ref[...]

    tm, tn = xn_ref.shape[0], w_ref.shape[1]
    ngrp = tn // LANES
    cw = min(tn, NORM_MATMUL_CHUNK)

    def run(act):
        for c0 in range(0, tn, cw):
            val = jnp.dot(xn_ref[...], w_ref[:, c0:c0 + cw], preferred_element_type=F32)
            if out_scale is not None:
                val = val * out_scale
            val = act(val, c0)
            for o, layout in zip(outs, layouts):
                if layout == "xn":
                    continue
                if layout == "rows":
                    o[:, c0:c0 + cw] = val.astype(o.dtype)
                    continue
                for gi in range(cw // LANES):
                    piece = val[:, gi * LANES:(gi + 1) * LANES].astype(o.dtype)
                    if layout == "groups":
                        o[c0 // LANES + gi] = piece
                    else:
                        o[pl.ds(c0 // LANES + gi, tm, stride=ngrp), :] = piece

    if not gate_tiles:
        run(lambda val, c0: val)
    else:
        kind = j // gate_tiles

        @pl.when((kind == 0) | (kind == 3))
        def _():
            run(lambda val, c0: _silu(val))

        @pl.when(kind == 1)
        def _():
            run(lambda val, c0: (1.0 - lb_ref[:, c0:c0 + cw]) / (1.0 + jnp.exp(val)))

        @pl.when(kind == 2)
        def _():
            run(lambda val, c0: val)


def norm_matmul(x, g, w, col0, ncols, outs, tm, tn, out_scale=None, hgrn_lb=None):
    m, d = x.shape
    assert m % tm == 0 and ncols % tn == 0 and col0 % tn == 0 and tn % LANES == 0
    joff = col0 // tn
    ngrp = tn // LANES
    prenorm = g is None
    assert (x.dtype == BF16) == prenorm
    g_in = [] if prenorm else [g.reshape(1, d)]
    g_spec = [] if prenorm else [pl.BlockSpec((1, d), lambda i, j: (0, 0))]
    extra_in, extra_specs, gate_tiles = [], [], 0
    if hgrn_lb is not None:
        quarter = ncols // 4
        assert quarter % tn == 0 and col0 == 0
        gate_tiles = quarter // tn
        zeros = jnp.zeros((quarter,), F32)
        extra_in = [jnp.concatenate([zeros, hgrn_lb, zeros, zeros]).reshape(1, ncols)]
        extra_specs = [pl.BlockSpec((1, tn), lambda i, j: (0, j))]
    specs, shapes = [], []
    for dt, layout in outs:
        if layout == "xn":
            assert dt == BF16 and not prenorm
            specs.append(pl.BlockSpec((tm, d), lambda i, j: (i, 0)))
            shapes.append(jax.ShapeDtypeStruct((m, d), dt))
        elif layout == "rows":
            specs.append(pl.BlockSpec((tm, tn), lambda i, j: (i, j)))
            shapes.append(jax.ShapeDtypeStruct((m, ncols), dt))
        elif layout == "groups":
            specs.append(pl.BlockSpec((ngrp, tm, LANES), lambda i, j: (j, i, 0)))
            shapes.append(jax.ShapeDtypeStruct((ncols // LANES, m, LANES), dt))
        else:
            assert layout == "interleaved" and tn == ncols and dt == F32
            specs.append(pl.BlockSpec((tm * ngrp, LANES), lambda i, j: (i, 0)))
            shapes.append(jax.ShapeDtypeStruct((m * ngrp, LANES), dt))
    return pl.pallas_call(
        functools.partial(_norm_matmul_kernel, layouts=tuple(l for _, l in outs),
                          out_scale=out_scale, gate_tiles=gate_tiles, prenorm=prenorm),
        grid=(m // tm, ncols // tn),
        in_specs=[pl.BlockSpec((tm, d), lambda i, j: (i, 0))] + g_spec
        + [pl.BlockSpec((d, tn), lambda i, j: (0, j + joff))] + extra_specs,
        out_specs=specs,
        out_shape=shapes,
        scratch_shapes=[] if prenorm else [pltpu.VMEM((tm, d), BF16)],
        compiler_params=_params("parallel", "arbitrary"),
        name="norm_matmul",
    )(x, *g_in, w, *extra_in)


def _matmul_res_kernel(a_ref, w_ref, r_ref, g_ref, o_ref, on_ref):
    out = r_ref[...] + jnp.dot(a_ref[...], w_ref[...], preferred_element_type=F32)
    o_ref[...] = out
    on_ref[...] = _rms(out, g_ref[...]).astype(on_ref.dtype)


def matmul_res(a, w, res, g, tm):
    m, k = a.shape
    n = w.shape[1]
    assert m % tm == 0
    return pl.pallas_call(
        _matmul_res_kernel,
        grid=(m // tm,),
        in_specs=[
            pl.BlockSpec((tm, k), lambda i: (i, 0)),
            pl.BlockSpec((k, n), lambda i: (0, 0)),
            pl.BlockSpec((tm, n), lambda i: (i, 0)),
            pl.BlockSpec((1, n), lambda i: (0, 0)),
        ],
        out_specs=[pl.BlockSpec((tm, n), lambda i: (i, 0)), pl.BlockSpec((tm, n), lambda i: (i, 0))],
        out_shape=[jax.ShapeDtypeStruct((m, n), F32), jax.ShapeDtypeStruct((m, n), BF16)],
        compiler_params=_params("parallel"),
        name="matmul_res",
    )(a, w, res, g.reshape(1, n))


def _ffn_kernel(x_ref, xn_ref, prev_ref, wg_ref, wv_ref, cw_ref, cb_ref, wd_ref, og_ref, *rest,
                final, nsplit):
    if final:
        y_ref, cs_ref, ext_ref, carry_ref = rest
    else:
        y_ref, yn_ref, cs_ref, ext_ref, carry_ref = rest
    l = pl.program_id(1)
    j = pl.program_id(2)
    nj = pl.num_programs(2)
    bs, lt, d = x_ref.shape
    tf = wg_ref.shape[1]
    halo = CONV_W - 1

    @pl.when(j == 0)
    def _():
        y_ref[...] = x_ref[...]

    @pl.when(l == 0)
    def _():
        ext_ref[0, :, 8 - halo:8, :] = prev_ref[...]

    @pl.when(l > 0)
    def _():
        ext_ref[0, :, 8 - halo:8, :] = carry_ref[j]

    lp = lt // nsplit
    rp = bs * lp
    last = None
    for s in range(nsplit):
        xn = xn_ref[:, s * lp:(s + 1) * lp, :].reshape(rp, d)
        gate = jnp.dot(xn, wg_ref[...], preferred_element_type=F32).reshape(bs, lp, tf)
        val = jnp.dot(xn, wv_ref[...], preferred_element_type=F32).reshape(bs, lp, tf)
        if s > 0:
            ext_ref[s, :, 8 - halo:8, :] = last
        ext_ref[s, :, 8:8 + lp, :] = gate
        last = gate[:, lp - halo:lp, :]
        c = cb_ref[...].reshape(1, 1, tf)
        for t in range(CONV_W):
            off = 8 - halo + t
            c = c + ext_ref[s, :, off:off + lp, :] * cw_ref[t:t + 1, :].reshape(1, 1, tf)
        act = (_silu(c) * val).reshape(rp, tf).astype(BF16)
        upd = jnp.dot(act, wd_ref[...], preferred_element_type=F32)
        y_ref[:, s * lp:(s + 1) * lp, :] += upd.reshape(bs, lp, d)
    carry_ref[j] = last
    cs_ref[...] = last

    @pl.when(j == nj - 1)
    def _():
        yn = _rms(y_ref[...], og_ref[...].reshape(1, 1, d))
        if final:
            y_ref[...] = yn
        else:
            yn_ref[...] = yn.astype(yn_ref.dtype)


def conv_ffn(x3, xn3, prev, w_up, conv_w, conv_b, w_down, layer, out_g, final, bs, lt, tf):
    nseq, L, d = x3.shape
    nlayers, dff, _ = w_down.shape
    assert nseq % bs == 0 and L % lt == 0 and dff % tf == 0 and lt % 8 == 0
    nj = dff // tf
    halo = CONV_W - 1
    nsplit = 2 if (bs == 1 and lt % 256 == 0) else 1
    row_tile = pl.BlockSpec((bs, lt, d), lambda s, l, j: (s, l, 0))
    tail_tile = pl.BlockSpec((bs, None, halo, tf), lambda s, l, j: (s, l, 0, j))
    y_shape = jax.ShapeDtypeStruct((nseq, L, d), F32)
    tail_shape = jax.ShapeDtypeStruct((nseq, L // lt, halo, dff), F32)
    outs = pl.pallas_call(
        functools.partial(_ffn_kernel, final=final, nsplit=nsplit),
        grid=(nseq // bs, L // lt, nj),
        in_specs=[
            row_tile,
            row_tile,
            pl.BlockSpec((bs, halo, tf), lambda s, l, j: (s, 0, j)),
            pl.BlockSpec((None, d, tf), lambda s, l, j: (layer, 0, j)),
            pl.BlockSpec((None, d, tf), lambda s, l, j: (layer, 0, j + nj)),
            pl.BlockSpec((None, CONV_W, tf), lambda s, l, j: (layer, 0, j)),
            pl.BlockSpec((None, 1, tf), lambda s, l, j: (layer, 0, j)),
            pl.BlockSpec((None, tf, d), lambda s, l, j: (layer, j, 0)),
            pl.BlockSpec((1, d), lambda s, l, j: (0, 0)),
        ],
        out_specs=[row_tile, tail_tile] if final else [row_tile, row_tile, tail_tile],
        out_shape=[y_shape, tail_shape] if final else
                  [y_shape, jax.ShapeDtypeStruct((nseq, L, d), BF16), tail_shape],
        scratch_shapes=[
            pltpu.VMEM((nsplit, bs, lt // nsplit + 8, tf), F32),
            pltpu.VMEM((nj, bs, halo, tf), F32),
        ],
        compiler_params=_params("parallel", "arbitrary", "arbitrary"),
        name="conv_ffn",
    )(x3, xn3, prev, w_up, w_up, conv_w, conv_b.reshape(nlayers, 1, dff), w_down, out_g.reshape(1, d))
    return tuple(outs[:-1]) + (outs[-1][:, -1],)


def _softmax_step(s, m, l, acc, vb):
    m_new = jnp.maximum(m, jnp.max(s, axis=-1, keepdims=True))
    alpha = jnp.exp2(m - m_new)
    p = jnp.exp2(s - m_new)
    l_new = alpha * l + jnp.sum(p, axis=-1, keepdims=True)
    acc_new = alpha * acc + jnp.dot(p.astype(BF16), vb, preferred_element_type=F32)
    return m_new, l_new, acc_new


def _subln(o, g, lam_init):
    return _rms(o, g) * (1.0 - lam_init)


def _attn_prompt_kernel(slopes_ref, lam_ref, q_ref, k_ref, v_ref, g_ref, o_ref,
                        vt_ref, own_ref, p_ref, acc_ref, *, lam_init):
    h = pl.program_id(1)
    qi = pl.program_id(2)
    t = q_ref.shape[0]
    dh = ATTN_HEAD_DIM
    slope2 = slopes_ref[h] * LOG2E
    lam = lam_ref[0]
    q0 = qi * t

    @pl.when(qi == 0)
    def _():
        vt_ref[...] = v_ref[...].T
        k_io = lax.broadcasted_iota(jnp.int32, (t, t), 0)
        q_io = lax.broadcasted_iota(jnp.int32, (t, t), 1)
        own_ref[...] = jnp.where((k_io // CHUNK) <= (q_io // CHUNK),
                                 slope2 * (q_io - jnp.abs(q_io - k_io)).astype(F32), NEG_BIG)

    q = q_ref[...]

    def probs(c, kb, bias, m, l):
        s = lax.dot_general(kb[:, c * dh:(c + 1) * dh], q[:, c * dh:(c + 1) * dh], _NT,
                            preferred_element_type=F32) + bias
        m_new = jnp.maximum(m, jnp.max(s, axis=0, keepdims=True))
        alpha = jnp.exp2(m - m_new)
        p = jnp.exp2(s - m_new)
        return m_new, alpha * l + jnp.sum(p, axis=0, keepdims=True), alpha, p.astype(BF16)

    def add_values(c, k0, alpha, p):
        acc_ref[c] = alpha * acc_ref[c] + jnp.dot(vt_ref[:, pl.ds(k0, t)], p,
                                                  preferred_element_type=F32)

    def tile(k0, prev0, bias, state):
        (m0, l0), (m1, l1, alpha1) = state
        kb = k_ref[pl.ds(k0, t), :]
        m0, l0, alpha0, p0 = probs(0, kb, bias, m0, l0)
        if prev0 is not None:
            add_values(1, prev0, alpha1, p_ref[...])
        m1, l1, alpha1, p1 = probs(1, kb, bias, m1, l1)
        p_ref[...] = p1
        add_values(0, k0, alpha0, p0)
        return (m0, l0), (m1, l1, alpha1)

    neg, zero = jnp.full((1, t), NEG_BIG, F32), jnp.zeros((1, t), F32)
    acc_ref[...] = jnp.zeros(acc_ref.shape, F32)
    own0 = pl.multiple_of(q0, t)
    state = tile(own0, None, own_ref[...], ((neg, zero), (neg, zero, zero)))

    key = lax.broadcasted_iota(jnp.int32, (t, LANES), 0)

    def past(kt, state):
        k0 = pl.multiple_of(kt * t, t)
        prev0 = pl.multiple_of(jnp.where(kt == 0, q0, k0 - t), t)
        bias = slope2 * (key + (k0 - q0)).astype(F32)
        return tile(k0, prev0, jnp.concatenate([bias] * (t // LANES), axis=1), state)

    (_, l0), (_, l1, alpha1) = lax.fori_loop(0, qi, past, state)
    add_values(1, pl.multiple_of(jnp.where(qi == 0, q0, q0 - t), t), alpha1, p_ref[...])
    a0, a1 = acc_ref[0], acc_ref[1]
    o = (a0 / l0 - lam * (a1 / l1)).T
    o_ref[...] = _subln(o, g_ref[...], lam_init).astype(o_ref.dtype)


def attn_prompt(q, k, v, slopes, lam, subln, lam_init, tq):
    b, t, _ = q.shape
    hw = 2 * ATTN_HEAD_DIM
    assert t % tq == 0 and tq % CHUNK == 0
    smem = pl.BlockSpec(memory_space=pltpu.SMEM)
    return pl.pallas_call(
        functools.partial(_attn_prompt_kernel, lam_init=lam_init),
        grid=(b, ATTN_HEADS, t // tq),
        in_specs=[
            smem, smem,
            pl.BlockSpec((None, tq, hw), lambda b, h, i: (b, i, h)),
            pl.BlockSpec((None, t, hw), lambda b, h, i: (b, 0, h)),
            pl.BlockSpec((None, t, hw), lambda b, h, i: (b, 0, h)),
            pl.BlockSpec((1, hw), lambda b, h, i: (0, 0)),
        ],
        out_specs=pl.BlockSpec((None, tq, hw), lambda b, h, i: (b, i, h)),
        out_shape=jax.ShapeDtypeStruct(q.shape, BF16),
        scratch_shapes=[
            pltpu.VMEM((hw, t), BF16),
            pltpu.VMEM((tq, tq), F32),
            pltpu.VMEM((tq, tq), BF16),
            pltpu.VMEM((2, hw, tq), F32),
        ],
        compiler_params=_params("parallel", "parallel", "arbitrary"),
        name="attn_prompt",
    )(slopes, lam, q, k, v, subln.reshape(1, hw))


def _attn_sample_kernel(slopes_ref, lam_ref, q_ref, kc_ref, vlo_ref, vhi_ref, kn_ref, vn_ref, g_ref,
                        o_ref, m_ref, l_ref, acc_ref, bias_ref, *, past, lam_init):
    kt = pl.program_id(1)
    nkt = pl.num_programs(1)
    nq = q_ref.shape[0]
    nh, ex = ATTN_HEADS, SAMPLE_EXPAND
    ng = nh // ex
    tk = vlo_ref.shape[0] // nh
    n = ex * tk
    rows = ex * nq
    dh = ATTN_HEAD_DIM
    hw = 2 * dh
    lam = lam_ref[0]

    def q_stack(g, c):
        return jnp.concatenate([q_ref[:, (g + ng * e) * hw + c * dh:(g + ng * e) * hw + (c + 1) * dh]
                                for e in range(ex)], axis=0)

    def scores(g, keys):
        return jnp.concatenate([lax.dot_general(q_stack(g, c), keys[c], _NT,
                                                preferred_element_type=F32) for c in range(2)], axis=0)

    def slope_rows(g):
        e_io = lax.broadcasted_iota(jnp.int32, (2 * rows, 1), 0) % rows // nq
        out = jnp.zeros((2 * rows, 1), F32)
        for e in range(ex):
            out = jnp.where(e_io == e, slopes_ref[g + ng * e] * LOG2E, out)
        return out

    def update(g, s, vb):
        m, l, acc = _softmax_step(s, m_ref[g][:, :1], l_ref[g][:, :1], acc_ref[g], vb)
        m_ref[g] = jnp.broadcast_to(m, m_ref.shape[1:])
        l_ref[g] = jnp.broadcast_to(l, l_ref.shape[1:])
        acc_ref[g] = acc

    @pl.when(kt == 0)
    def _():
        m_ref[...] = jnp.full(m_ref.shape, NEG_BIG, F32)
        l_ref[...] = jnp.zeros(l_ref.shape, F32)
        acc_ref[...] = jnp.zeros(acc_ref.shape, F32)
        e_row = lax.broadcasted_iota(jnp.int32, (2 * rows, n), 0) % rows // nq
        c_io = lax.broadcasted_iota(jnp.int32, (2 * rows, n), 1)
        for g in range(ng):
            bias_ref[g] = jnp.where(e_row == c_io % ex,
                                    slope_rows(g) * (c_io // ex - past).astype(F32), NEG_BIG)

    for g in range(ng):
        keys = [kc_ref[pl.ds(2 * g + c, n, stride=2 * ng), :].astype(BF16) for c in range(2)]
        vals = jnp.concatenate([vlo_ref[pl.ds(g, n, stride=ng), :], vhi_ref[pl.ds(g, n, stride=ng), :]],
                               axis=1).astype(BF16)
        update(g, scores(g, keys) + (bias_ref[g] + slope_rows(g) * (kt * tk).astype(F32)), vals)

    @pl.when(kt == nkt - 1)
    def _():
        r_io = lax.broadcasted_iota(jnp.int32, (2 * rows, rows), 0) % rows
        c_io = lax.broadcasted_iota(jnp.int32, (2 * rows, rows), 1)
        tok, key = r_io % nq, c_io % nq
        rel_n = (tok - jnp.abs(tok - key)).astype(F32)
        allowed = (r_io // nq == c_io // nq) & (((key + past) // CHUNK) <= ((tok + past) // CHUNK))
        for g in range(ng):
            heads = [g + ng * e for e in range(ex)]
            keys = [jnp.concatenate([kn_ref[:, h * hw + c * dh:h * hw + (c + 1) * dh] for h in heads],
                                    axis=0) for c in range(2)]
            vals = jnp.concatenate([vn_ref[:, h * hw:(h + 1) * hw] for h in heads], axis=0)
            update(g, jnp.where(allowed, scores(g, keys) + slope_rows(g) * rel_n, NEG_BIG), vals)
            on = acc_ref[g] / l_ref[g][:, :1]
            for e, h in enumerate(heads):
                o = on[e * nq:(e + 1) * nq] - lam * on[rows + e * nq:rows + (e + 1) * nq]
                o_ref[:, h * hw:(h + 1) * hw] = _subln(o, g_ref[...], lam_init).astype(o_ref.dtype)


def attn_sample(q, k_new, v_new, k_cache, v_cache, slopes, lam, subln, lam_init, tk):
    b, nq, d = q.shape
    nh, dh = ATTN_HEADS, ATTN_HEAD_DIM
    past = v_cache.shape[1] // nh
    hw = 2 * dh
    ex = SAMPLE_EXPAND
    ng = nh // ex
    assert past % tk == 0 and past > 0 and dh == LANES and nh % ex == 0
    smem = pl.BlockSpec(memory_space=pltpu.SMEM)
    return pl.pallas_call(
        functools.partial(_attn_sample_kernel, past=past, lam_init=lam_init),
        grid=(b, past // tk),
        in_specs=[
            smem, smem,
            pl.BlockSpec((None, nq, d), lambda b, k: (b, 0, 0)),
            pl.BlockSpec((None, tk * nh * 2, dh), lambda b, k: (b, k, 0)),
            pl.BlockSpec((None, tk * nh, dh), lambda b, k: (b, k, 0)),
            pl.BlockSpec((None, tk * nh, dh), lambda b, k: (b, k, 1)),
            pl.BlockSpec((None, nq, d), lambda b, k: (b, 0, 0)),
            pl.BlockSpec((None, nq, d), lambda b, k: (b, 0, 0)),
            pl.BlockSpec((1, hw), lambda b, k: (0, 0)),
        ],
        out_specs=pl.BlockSpec((None, nq, d), lambda b, k: (b, 0, 0)),
        out_shape=jax.ShapeDtypeStruct(q.shape, BF16),
        scratch_shapes=[
            pltpu.VMEM((ng, 2 * ex * nq, LANES), F32),
            pltpu.VMEM((ng, 2 * ex * nq, LANES), F32),
            pltpu.VMEM((ng, 2 * ex * nq, hw), F32),
            pltpu.VMEM((ng, 2 * ex * nq, ex * tk), F32),
        ],
        compiler_params=_params("parallel", "arbitrary"),
        name="attn_sample",
    )(slopes, lam, q, k_cache, v_cache, v_cache, k_new, v_new, subln.reshape(1, hw))


def _cumsum_rows(tri, x):
    hi = x.astype(BF16)
    r1 = x - hi.astype(F32)
    mid = r1.astype(BF16)
    lo = (r1 - mid.astype(F32)).astype(BF16)
    n = x.shape[1]
    y = jnp.dot(tri, jnp.concatenate([hi, mid, lo], axis=1), preferred_element_type=F32)
    return y[:, :n] + y[:, n:2 * n] + y[:, 2 * n:]


def _hgrn_kernel(q_ref, k_ref, i_ref, g_ref, gn_ref, s0_ref, o_ref, sout_ref,
                 s_ref, qb_ref, kb_ref, *, chunk, sub):
    tb = pl.program_id(2)
    hb, tbs, dk = q_ref.shape
    nch = tbs // chunk
    nsub = chunk // sub

    @pl.when(tb == 0)
    def _():
        s_ref[...] = s0_ref[...]
        qb_ref[...] = jnp.zeros(qb_ref.shape, BF16)
        kb_ref[...] = jnp.zeros(kb_ref.shape, BF16)

    r_io = lax.broadcasted_iota(jnp.int32, (chunk, chunk), 0)
    c_io = lax.broadcasted_iota(jnp.int32, (chunk, chunk), 1)
    causal = c_io <= r_io
    tri = causal.astype(BF16)

    for h in range(hb):
        parts = []
        for c in range(nch):
            t0, slot = c * chunk, h * nch + c
            kk = k_ref[h, t0:t0 + chunk, :]
            qs = q_ref[h, t0:t0 + chunk, :]
            v = i_ref[h, t0:t0 + chunk, :].astype(BF16)
            b = _cumsum_rows(tri, jnp.log(1.0 - kk))
            rows = [slice(sub * j, sub * (j + 1)) for j in range(nsub)]
            ends = [b[sub * (j + 1) - 1:sub * (j + 1), :] for j in range(nsub)]
            b_last = ends[-1]
            k_own = [kk[rows[j]] * jnp.exp(ends[j] - b[rows[j]]) for j in range(nsub)]
            q_end, k_end = [], []
            for i in range(nsub):
                lanes = slice(dk * i, dk * (i + 1))
                ref = ends[i - 1] if i > 0 else jnp.zeros((1, dk), F32)
                q_rel = qs[rows[i]] * jnp.exp(b[rows[i]] - ref)
                qb_ref[slot, rows[i], lanes] = q_rel.astype(BF16)
                q_end.append(q_rel * jnp.exp(ref))
                for j in range(i):
                    k_rel = k_own[j] if j == i - 1 else k_own[j] * jnp.exp(ref - ends[j])
                    kb_ref[slot, rows[j], lanes] = k_rel.astype(BF16)
                kb_ref[slot, rows[i], lanes] = (
                    kk[rows[i]] * jnp.exp(jnp.minimum(ref - b[rows[i]], EXP_CLAMP))).astype(BF16)
                k_end.append(k_own[i] if i == nsub - 1 else k_own[i] * jnp.exp(b_last - ends[i]))
            a = lax.dot_general(qb_ref[slot], kb_ref[slot], _NT, preferred_element_type=F32)
            a = jnp.where(causal, a, 0.0).astype(BF16)
            q_end = jnp.concatenate(q_end, axis=0).astype(BF16)
            k_end = jnp.concatenate(k_end, axis=0).astype(BF16)
            upd = lax.dot_general(k_end, v, _TN, preferred_element_type=F32)
            decay = jnp.transpose(jnp.broadcast_to(jnp.exp(b_last), (REC_DV, dk)))
            parts.append((jnp.concatenate([a, q_end], axis=1), v, decay, upd))
        state = s_ref[h]
        for c in range(nch):
            t0 = c * chunk
            a_q, v, decay, upd = parts[c]
            o = jnp.dot(a_q, jnp.concatenate([v, state.astype(BF16)], axis=0),
                        preferred_element_type=F32)
            state = decay * state + upd
            y = _rms(o, gn_ref[...]) * g_ref[h, t0:t0 + chunk, :]
            o_ref[t0:t0 + chunk, h * REC_DV:(h + 1) * REC_DV] = y.astype(o_ref.dtype)
        s_ref[h] = state
    sout_ref[...] = s_ref[...]


def hgrn(qfig, s0, out_norm, hb, tbs, chunk, sub):
    _, nh, b, t, dk = qfig.shape
    dv = REC_DV
    assert t % tbs == 0 and tbs % chunk == 0 and chunk % sub == 0 and sub % 16 == 0 and nh % hb == 0
    nsub = chunk // sub
    nslot = hb * (tbs // chunk)

    def kind(k):
        return pl.BlockSpec((None, hb, None, tbs, dk), lambda b, h, i: (k, h, b, i, 0))

    return pl.pallas_call(
        functools.partial(_hgrn_kernel, chunk=chunk, sub=sub),
        grid=(b, nh // hb, t // tbs),
        in_specs=[
            kind(0), kind(1), kind(2), kind(3),
            pl.BlockSpec((1, dv), lambda b, h, i: (0, 0)),
            pl.BlockSpec((None, hb, dk, dv), lambda b, h, i: (b, h, 0, 0)),
        ],
        out_specs=[
            pl.BlockSpec((None, tbs, hb * dv), lambda b, h, i: (b, i, h)),
            pl.BlockSpec((None, hb, dk, dv), lambda b, h, i: (b, h, 0, 0)),
        ],
        out_shape=[
            jax.ShapeDtypeStruct((b, t, nh * dv), BF16),
            jax.ShapeDtypeStruct((b, nh, dk, dv), F32),
        ],
        scratch_shapes=[
            pltpu.VMEM((hb, dk, dv), F32),
            pltpu.VMEM((nslot, chunk, nsub * dk), BF16),
            pltpu.VMEM((nslot, chunk, nsub * dk), BF16),
        ],
        compiler_params=_params("parallel", "parallel", "arbitrary"),
        name="hgrn",
    )(qfig, qfig, qfig, qfig, out_norm.reshape(1, dv), s0)


def _pick(n, candidates):
    for c in candidates:
        if n % c == 0:
            return c
    return n


def _trunk(x3, is_prompt, cache_k, cache_v, state_hgrn, state_conv, p):
    nseq, L, d = x3.shape
    m = nseq * L
    dff = p["ffn_w_down"].shape[1]
    tm = _pick(m, (1024, 512, 256))
    tf = _pick(dff, (512, 256, 128))
    if is_prompt:
        ffn_bs, ffn_lt = 1, _pick(L, (512, 256, 128))
    else:
        ffn_bs, ffn_lt = nseq, L
    x = x3.reshape(m, d)

    wqkv = p["attn_w_qkv"][0]
    nqk = ATTN_HEADS * 2 * ATTN_HEAD_DIM
    g0 = p["mixer_norm"][0]
    tmw = _pick(m, (512, 256))
    q, xn = norm_matmul(x, g0, wqkv, 0, nqk, ((BF16, "rows"), (BF16, "xn")), tmw, nqk,
                        out_scale=ATTN_SCALE * LOG2E)
    k32, k16 = norm_matmul(xn, None, wqkv, nqk, nqk, ((F32, "interleaved"), (BF16, "rows")), tmw, nqk)
    v32, v16 = norm_matmul(xn, None, wqkv, 2 * nqk, nqk, ((F32, "rows"), (BF16, "rows")), tmw, nqk)
    sh = (nseq, L, nqk)
    lam_init = 0.8 - 0.6 * math.exp(-0.3 * 0)
    if is_prompt:
        o = attn_prompt(q.reshape(sh), k16.reshape(sh), v16.reshape(sh), p["slopes"], p["lam"],
                        p["attn_subln"][0], lam_init, tq=_pick(L, (512, 256, 128, 64)))
    else:
        past = cache_k.shape[2]
        o = attn_sample(q.reshape(sh), k16.reshape(sh), v16.reshape(sh),
                        cache_k[0].reshape(nseq, past * ATTN_HEADS * 2, ATTN_HEAD_DIM),
                        cache_v[0].reshape(nseq, past * ATTN_HEADS, ATTN_V_DIM),
                        p["slopes"], p["lam"], p["attn_subln"][0], lam_init,
                        tk=_pick(past, (512, 256, 128)))
    x, xn = matmul_res(o.reshape(m, nqk), p["attn_w_o"][0], x, p["ffn_norm"][0], tmw)
    prev0 = jnp.zeros((nseq, CONV_W - 1, dff), F32) if is_prompt else state_conv[0]
    ffn_w = (p["ffn_w_up"], p["ffn_conv_w"], p["ffn_conv_b"], p["ffn_w_down"])
    x3b, xn3, cs0 = conv_ffn(x.reshape(nseq, L, d), xn.reshape(nseq, L, d), prev0, *ffn_w, 0,
                             p["mixer_norm"][1], False, ffn_bs, ffn_lt, tf)
    x = x3b.reshape(m, d)

    wq = p["rec_w_qfig"][0]
    (qfig,) = norm_matmul(xn3.reshape(m, d), None, wq, 0, wq.shape[1], ((F32, "groups"),), tm, 512,
                          hgrn_lb=p["lb"])
    if is_prompt:
        s0 = jnp.zeros((nseq, REC_HEADS, REC_DK, REC_DV), F32)
        chunk = _pick(L, (128, 64, 32, 16))
        hb, tbs, sub = 2, _pick(L, (1024, 512, 256, 128, 64, 32, 16)), 16
    else:
        s0 = state_hgrn[0]
        hb, chunk, tbs, sub = REC_HEADS, L, L, 16
    o, s_fin = hgrn(qfig.reshape(4, REC_HEADS, nseq, L, REC_DK), s0, p["rec_out_norm"][0],
                    hb, tbs, chunk, sub)
    x, xn = matmul_res(o.reshape(m, d), p["rec_w_o"][0], x, p["ffn_norm"][1], tmw)
    prev1 = jnp.zeros((nseq, CONV_W - 1, dff), F32) if is_prompt else state_conv[1]
    y, cs1 = conv_ffn(x.reshape(nseq, L, d), xn.reshape(nseq, L, d), prev1, *ffn_w, 1,
                      p["final_norm"], True, ffn_bs, ffn_lt, tf)
    kshape = (1, nseq, L, ATTN_HEADS, 2, ATTN_HEAD_DIM)
    vshape = (1, nseq, L, ATTN_HEADS, ATTN_V_DIM)
    return (y, k32.reshape(kshape), v32.reshape(vshape), s_fin[None], jnp.stack([cs0, cs1]))


def kernel(x_prompt, x_sample, cache_k, cache_v, state_hgrn, state_conv, mixer_norm, ffn_norm,
           attn_w_qkv, attn_lambda_q1, attn_lambda_k1, attn_lambda_q2, attn_lambda_k2, attn_subln,
           attn_w_o, rec_w_qfig, rec_lower_bounds, rec_out_norm, rec_w_o, ffn_w_up, ffn_conv_w,
           ffn_conv_b, ffn_w_down, final_norm):
    lam_init = 0.8 - 0.6 * math.exp(-0.3 * 0)
    lam = (jnp.exp(jnp.sum(attn_lambda_q1[0] * attn_lambda_k1[0]))
           - jnp.exp(jnp.sum(attn_lambda_q2[0] * attn_lambda_k2[0])) + lam_init).reshape(1)
    slopes = jnp.exp2(-8.0 * jnp.arange(1, ATTN_HEADS + 1, dtype=F32) / ATTN_HEADS)
    lbs = jax.nn.softmax(rec_lower_bounds.astype(F32), axis=0)
    lb = (jnp.cumsum(lbs, axis=0) - lbs[0])[1]
    p = dict(
        mixer_norm=mixer_norm, ffn_norm=ffn_norm, attn_subln=attn_subln, rec_out_norm=rec_out_norm,
        ffn_conv_w=ffn_conv_w, ffn_conv_b=ffn_conv_b, final_norm=final_norm,
        lam=lam, slopes=slopes, lb=lb,
        attn_w_qkv=attn_w_qkv.astype(BF16), attn_w_o=attn_w_o.astype(BF16),
        rec_w_qfig=rec_w_qfig.astype(BF16), rec_w_o=rec_w_o.astype(BF16),
        ffn_w_up=ffn_w_up.astype(BF16), ffn_w_down=ffn_w_down.astype(BF16),
    )
    yp, kp, vp, sp, cp = _trunk(x_prompt, True, None, None, None, None, p)
    ys, ks, vs, ss, cs = _trunk(x_sample, False, cache_k, cache_v, state_hgrn, state_conv, p)
    return (yp, ys, kp, vp, ks, vs, sp, ss, cp, cs)
```

```python
import functools
import math

import jax
import jax.numpy as jnp
from jax import lax
from jax.experimental import pallas as pl
from jax.experimental.pallas import tpu as pltpu

F32 = jnp.float32
BF16 = jnp.bfloat16

CHUNK = 64
EPS = 1e-6
ATTN_HEADS = 8
ATTN_HEAD_DIM = 128
ATTN_V_DIM = 2 * ATTN_HEAD_DIM
ATTN_SCALE = ATTN_HEAD_DIM ** -0.5
LOG2E = math.log2(math.e)
REC_HEADS = 16
REC_DK = 128
REC_DV = 128
CONV_W = 3
NEG_BIG = -1e30
EXP_CLAMP = 80.0

V7X_VMEM_LIMIT = 56 * 1024 * 1024

_NT = (((1,), (1,)), ((), ()))
_TN = (((0,), (0,)), ((), ()))


def _params(*sem):
    return pltpu.CompilerParams(dimension_semantics=sem, vmem_limit_bytes=V7X_VMEM_LIMIT)


def _rms(x, g):
    ms = jnp.mean(x * x, axis=-1, keepdims=True)
    return (x * lax.rsqrt(ms + EPS)) * g


def _silu(x):
    return x / (1.0 + jnp.exp(-x))


LANES = 128
NORM_MATMUL_CHUNK = 256
SAMPLE_EXPAND = 4


def _norm_matmul_kernel(x_ref, *rest, layouts, out_scale, gate_tiles, prenorm):
    if not prenorm:
        g_ref, rest = rest[0], rest[1:]
    w_ref, rest = rest[0], rest[1:]
    if gate_tiles:
        lb_ref, rest = rest[0], rest[1:]
    outs = rest[:len(layouts)]
    xn_ref = x_ref if prenorm else rest[len(layouts)]
    j = pl.program_id(1)

    if not prenorm:
        @pl.when(j == 0)
        def _():
            xn_ref[...] = _rms(x_ref[...], g_ref[...]).astype(BF16)
            for o, layout in zip(outs, layouts):
                if layout == "xn":
                    o[...] = xn_ref[...]

    tm, tn = xn_ref.shape[0], w_ref.shape[1]
    ngrp = tn // LANES
    cw = min(tn, NORM_MATMUL_CHUNK)

    def run(act):
        for c0 in range(0, tn, cw):
            val = jnp.dot(xn_ref[...], w_ref[:, c0:c0 + cw], preferred_element_type=F32)
            if out_scale is not None:
                val = val * out_scale
            val = act(val, c0)
            for o, layout in zip(outs, layouts):
                if layout == "xn":
                    continue
                if layout == "rows":
                    o[:, c0:c0 + cw] = val.astype(o.dtype)
                    continue
                for gi in range(cw // LANES):
                    piece = val[:, gi * LANES:(gi + 1) * LANES].astype(o.dtype)
                    if layout == "groups":
                        o[c0 // LANES + gi] = piece
                    else:
                        o[pl.ds(c0 // LANES + gi, tm, stride=ngrp), :] = piece

    if not gate_tiles:
        run(lambda val, c0: val)
    else:
        kind = j // gate_tiles

        @pl.when((kind == 0) | (kind == 3))
        def _():
            run(lambda val, c0: _silu(val))

        @pl.when(kind == 1)
        def _():
            run(lambda val, c0: (1.0 - lb_ref[:, c0:c0 + cw]) / (1.0 + jnp.exp(val)))

        @pl.when(kind == 2)
        def _():
            run(lambda val, c0: val)


def norm_matmul(x, g, w, col0, ncols, outs, tm, tn, out_scale=None, hgrn_lb=None):
    m, d = x.shape
    assert m % tm == 0 and ncols % tn == 0 and col0 % tn == 0 and tn % LANES == 0
    joff = col0 // tn
    ngrp = tn // LANES
    prenorm = g is None
    assert (x.dtype == BF16) == prenorm
    g_in = [] if prenorm else [g.reshape(1, d)]
    g_spec = [] if prenorm else [pl.BlockSpec((1, d), lambda i, j: (0, 0))]
    extra_in, extra_specs, gate_tiles = [], [], 0
    if hgrn_lb is not None:
        quarter = ncols // 4
        assert quarter % tn == 0 and col0 == 0
        gate_tiles = quarter // tn
        zeros = jnp.zeros((quarter,), F32)
        extra_in = [jnp.concatenate([zeros, hgrn_lb, zeros, zeros]).reshape(1, ncols)]
        extra_specs = [pl.BlockSpec((1, tn), lambda i, j: (0, j))]
    specs, shapes = [], []
    for dt, layout in outs:
        if layout == "xn":
            assert dt == BF16 and not prenorm
            specs.append(pl.BlockSpec((tm, d), lambda i, j: (i, 0)))
            shapes.append(jax.ShapeDtypeStruct((m, d), dt))
        elif layout == "rows":
            specs.append(pl.BlockSpec((tm, tn), lambda i, j: (i, j)))
            shapes.append(jax.ShapeDtypeStruct((m, ncols), dt))
        elif layout == "groups":
            specs.append(pl.BlockSpec((ngrp, tm, LANES), lambda i, j: (j, i, 0)))
            shapes.append(jax.ShapeDtypeStruct((ncols // LANES, m, LANES), dt))
        else:
            assert layout == "interleaved" and tn == ncols and dt == F32
            specs.append(pl.BlockSpec((tm * ngrp, LANES), lambda i, j: (i, 0)))
            shapes.append(jax.ShapeDtypeStruct((m * ngrp, LANES), dt))
    return pl.pallas_call(
        functools.partial(_norm_matmul_kernel, layouts=tuple(l for _, l in outs),
                          out_scale=out_scale, gate_tiles=gate_tiles, prenorm=prenorm),
        grid=(m // tm, ncols // tn),
        in_specs=[pl.BlockSpec((tm, d), lambda i, j: (i, 0))] + g_spec
        + [pl.BlockSpec((d, tn), lambda i, j: (0, j + joff))] + extra_specs,
        out_specs=specs,
        out_shape=shapes,
        scratch_shapes=[] if prenorm else [pltpu.VMEM((tm, d), BF16)],
        compiler_params=_params("parallel", "arbitrary"),
        name="norm_matmul",
    )(x, *g_in, w, *extra_in)


def _matmul_res_kernel(a_ref, w_ref, r_ref, g_ref, o_ref, on_ref):
    out = r_ref[...] + jnp.dot(a_ref[...], w_ref[...], preferred_element_type=F32)
    o_ref[...] = out
    on_ref[...] = _rms(out, g_ref[...]).astype(on_ref.dtype)


def matmul_res(a, w, res, g, tm):
    m, k = a.shape
    n = w.shape[1]
    assert m % tm == 0
    return pl.pallas_call(
        _matmul_res_kernel,
        grid=(m // tm,),
        in_specs=[
            pl.BlockSpec((tm, k), lambda i: (i, 0)),
            pl.BlockSpec((k, n), lambda i: (0, 0)),
            pl.BlockSpec((tm, n), lambda i: (i, 0)),
            pl.BlockSpec((1, n), lambda i: (0, 0)),
        ],
        out_specs=[pl.BlockSpec((tm, n), lambda i: (i, 0)), pl.BlockSpec((tm, n), lambda i: (i, 0))],
        out_shape=[jax.ShapeDtypeStruct((m, n), F32), jax.ShapeDtypeStruct((m, n), BF16)],
        compiler_params=_params("parallel"),
        name="matmul_res",
    )(a, w, res, g.reshape(1, n))


def _ffn_kernel(x_ref, xn_ref, prev_ref, wg_ref, wv_ref, cw_ref, cb_ref, wd_ref, og_ref, *rest,
                final, nsplit):
    rest = list(rest)
    y_ref = rest.pop(0)
    yn_ref = None if final else rest.pop(0)
    cs_ref = rest.pop(0)
    emit_w = wg_ref.dtype != BF16
    if emit_w:
        wg16_ref, wv16_ref, wd16_ref = rest.pop(0), rest.pop(0), rest.pop(0)
    ext_ref, carry_ref = rest
    l = pl.program_id(1)
    j = pl.program_id(2)
    nj = pl.num_programs(2)
    bs, lt, d = x_ref.shape
    tf = wg_ref.shape[1]
    halo = CONV_W - 1

    @pl.when(j == 0)
    def _():
        y_ref[...] = x_ref[...]

    @pl.when(l == 0)
    def _():
        ext_ref[0, :, 8 - halo:8, :] = prev_ref[...]

    @pl.when(l > 0)
    def _():
        ext_ref[0, :, 8 - halo:8, :] = carry_ref[j]

    lp = lt // nsplit
    rp = bs * lp
    last = None
    if emit_w:
        wg, wv, wd = (r[...].astype(BF16) for r in (wg_ref, wv_ref, wd_ref))
        wg16_ref[...], wv16_ref[...], wd16_ref[...] = wg, wv, wd
    for s in range(nsplit):
        xn = xn_ref[:, s * lp:(s + 1) * lp, :].reshape(rp, d)
        if not emit_w:
            wg, wv, wd = wg_ref[...], wv_ref[...], wd_ref[...]
        gate = jnp.dot(xn, wg, preferred_element_type=F32).reshape(bs, lp, tf)
        val = jnp.dot(xn, wv, preferred_element_type=F32).reshape(bs, lp, tf)
        if s > 0:
            ext_ref[s, :, 8 - halo:8, :] = last
        ext_ref[s, :, 8:8 + lp, :] = gate
        last = gate[:, lp - halo:lp, :]
        c = cb_ref[...].reshape(1, 1, tf)
        for t in range(CONV_W):
            off = 8 - halo + t
            c = c + ext_ref[s, :, off:off + lp, :] * cw_ref[t:t + 1, :].reshape(1, 1, tf)
        act = (_silu(c) * val).reshape(rp, tf).astype(BF16)
        upd = jnp.dot(act, wd, preferred_element_type=F32)
        y_ref[:, s * lp:(s + 1) * lp, :] += upd.reshape(bs, lp, d)
    carry_ref[j] = last
    cs_ref[...] = last

    @pl.when(j == nj - 1)
    def _():
        yn = _rms(y_ref[...], og_ref[...].reshape(1, 1, d))
        if final:
            y_ref[...] = yn
        else:
            yn_ref[...] = yn.astype(yn_ref.dtype)


def conv_ffn(x3, xn3, prev, w_gate, w_val, val_col0, w_down, layer, conv_w, conv_b, conv_layer,
             out_g, final, bs, lt, tf):
    nseq, L, d = x3.shape
    nlayers, dff, _ = w_down.shape
    assert nseq % bs == 0 and L % lt == 0 and dff % tf == 0 and lt % 8 == 0 and val_col0 % tf == 0
    nj = dff // tf
    voff = val_col0 // tf
    emit_w = w_down.dtype != BF16
    assert not emit_w or (nseq == bs and L == lt)
    halo = CONV_W - 1
    nsplit = 2 if (bs == 1 and lt % 256 == 0) else 1
    row_tile = pl.BlockSpec((bs, lt, d), lambda s, l, j: (s, l, 0))
    tail_tile = pl.BlockSpec((bs, None, halo, tf), lambda s, l, j: (s, l, 0, j))
    y_shape = jax.ShapeDtypeStruct((nseq, L, d), F32)
    tail_shape = jax.ShapeDtypeStruct((nseq, L // lt, halo, dff), F32)
    out_specs = [row_tile] + ([] if final else [row_tile]) + [tail_tile]
    out_shape = [y_shape] + ([] if final else [jax.ShapeDtypeStruct((nseq, L, d), BF16)]) + [tail_shape]
    if emit_w:
        out_specs += [pl.BlockSpec((None, d, tf), lambda s, l, j: (0, 0, j)),
                      pl.BlockSpec((None, d, tf), lambda s, l, j: (0, 0, j)),
                      pl.BlockSpec((None, tf, d), lambda s, l, j: (0, j, 0))]
        out_shape += [jax.ShapeDtypeStruct((1, d, dff), BF16), jax.ShapeDtypeStruct((1, d, dff), BF16),
                      jax.ShapeDtypeStruct((1, dff, d), BF16)]
    n_main = 2 if final else 3
    outs = pl.pallas_call(
        functools.partial(_ffn_kernel, final=final, nsplit=nsplit),
        grid=(nseq // bs, L // lt, nj),
        in_specs=[
            row_tile,
            row_tile,
            pl.BlockSpec((bs, halo, tf), lambda s, l, j: (s, 0, j)),
            pl.BlockSpec((None, d, tf), lambda s, l, j: (layer, 0, j)),
            pl.BlockSpec((None, d, tf), lambda s, l, j: (layer, 0, j + voff)),
            pl.BlockSpec((None, CONV_W, tf), lambda s, l, j: (conv_layer, 0, j)),
            pl.BlockSpec((None, 1, tf), lambda s, l, j: (conv_layer, 0, j)),
            pl.BlockSpec((None, tf, d), lambda s, l, j: (layer, j, 0)),
            pl.BlockSpec((1, d), lambda s, l, j: (0, 0)),
        ],
        out_specs=out_specs,
        out_shape=out_shape,
        scratch_shapes=[
            pltpu.VMEM((nsplit, bs, lt // nsplit + 8, tf), F32),
            pltpu.VMEM((nj, bs, halo, tf), F32),
        ],
        compiler_params=_params("parallel", "arbitrary", "arbitrary"),
        name="conv_ffn",
    )(x3, xn3, prev, w_gate, w_val, conv_w, conv_b.reshape(-1, 1, dff), w_down,
      out_g.reshape(1, d))
    return tuple(outs[:n_main - 1]) + (outs[n_main - 1][:, -1],) + tuple(outs[n_main:])


def _softmax_step(s, m, l, acc, vb):
    m_new = jnp.maximum(m, jnp.max(s, axis=-1, keepdims=True))
    alpha = jnp.exp2(m - m_new)
    p = jnp.exp2(s - m_new)
    l_new = alpha * l + jnp.sum(p, axis=-1, keepdims=True)
    acc_new = alpha * acc + jnp.dot(p.astype(BF16), vb, preferred_element_type=F32)
    return m_new, l_new, acc_new


def _subln(o, g, lam_init):
    return _rms(o, g) * (1.0 - lam_init)


def _attn_prompt_kernel(slopes_ref, lam_ref, q_ref, k_ref, v_ref, g_ref, o_ref,
                        vt_ref, own_ref, p_ref, acc_ref, *, lam_init):
    h = pl.program_id(1)
    qi = pl.program_id(2)
    t = q_ref.shape[0]
    dh = ATTN_HEAD_DIM
    slope2 = slopes_ref[h] * LOG2E
    lam = lam_ref[0]
    q0 = qi * t

    @pl.when(qi == 0)
    def _():
        vt_ref[...] = v_ref[...].T
        k_io = lax.broadcasted_iota(jnp.int32, (t, t), 0)
        q_io = lax.broadcasted_iota(jnp.int32, (t, t), 1)
        own_ref[...] = jnp.where((k_io // CHUNK) <= (q_io // CHUNK),
                                 slope2 * (q_io - jnp.abs(q_io - k_io)).astype(F32), NEG_BIG)

    q = q_ref[...]

    def probs(c, kb, bias, m, l):
        s = lax.dot_general(kb[:, c * dh:(c + 1) * dh], q[:, c * dh:(c + 1) * dh], _NT,
                            preferred_element_type=F32) + bias
        m_new = jnp.maximum(m, jnp.max(s, axis=0, keepdims=True))
        alpha = jnp.exp2(m - m_new)
        p = jnp.exp2(s - m_new)
        return m_new, alpha * l + jnp.sum(p, axis=0, keepdims=True), alpha, p.astype(BF16)

    def add_values(c, k0, alpha, p):
        acc_ref[c] = alpha * acc_ref[c] + jnp.dot(vt_ref[:, pl.ds(k0, t)], p,
                                                  preferred_element_type=F32)

    def tile(k0, prev0, bias, state):
        (m0, l0), (m1, l1, alpha1) = state
        kb = k_ref[pl.ds(k0, t), :]
        m0, l0, alpha0, p0 = probs(0, kb, bias, m0, l0)
        if prev0 is not None:
            add_values(1, prev0, alpha1, p_ref[...])
        m1, l1, alpha1, p1 = probs(1, kb, bias, m1, l1)
        p_ref[...] = p1
        add_values(0, k0, alpha0, p0)
        return (m0, l0), (m1, l1, alpha1)

    neg, zero = jnp.full((1, t), NEG_BIG, F32), jnp.zeros((1, t), F32)
    acc_ref[...] = jnp.zeros(acc_ref.shape, F32)
    own0 = pl.multiple_of(q0, t)
    state = tile(own0, None, own_ref[...], ((neg, zero), (neg, zero, zero)))

    key = lax.broadcasted_iota(jnp.int32, (t, LANES), 0)

    def past(kt, state):
        k0 = pl.multiple_of(kt * t, t)
        prev0 = pl.multiple_of(jnp.where(kt == 0, q0, k0 - t), t)
        bias = slope2 * (key + (k0 - q0)).astype(F32)
        return tile(k0, prev0, jnp.concatenate([bias] * (t // LANES), axis=1), state)

    (_, l0), (_, l1, alpha1) = lax.fori_loop(0, qi, past, state)
    add_values(1, pl.multiple_of(jnp.where(qi == 0, q0, q0 - t), t), alpha1, p_ref[...])
    a0, a1 = acc_ref[0], acc_ref[1]
    o = (a0 / l0 - lam * (a1 / l1)).T
    o_ref[...] = _subln(o, g_ref[...], lam_init).astype(o_ref.dtype)


def attn_prompt(q, k, v, slopes, lam, subln, lam_init, tq):
    b, t, _ = q.shape
    hw = 2 * ATTN_HEAD_DIM
    assert t % tq == 0 and tq % CHUNK == 0
    smem = pl.BlockSpec(memory_space=pltpu.SMEM)
    return pl.pallas_call(
        functools.partial(_attn_prompt_kernel, lam_init=lam_init),
        grid=(b, ATTN_HEADS, t // tq),
        in_specs=[
            smem, smem,
            pl.BlockSpec((None, tq, hw), lambda b, h, i: (b, i, h)),
            pl.BlockSpec((None, t, hw), lambda b, h, i: (b, 0, h)),
            pl.BlockSpec((None, t, hw), lambda b, h, i: (b, 0, h)),
            pl.BlockSpec((1, hw), lambda b, h, i: (0, 0)),
        ],
        out_specs=pl.BlockSpec((None, tq, hw), lambda b, h, i: (b, i, h)),
        out_shape=jax.ShapeDtypeStruct(q.shape, BF16),
        scratch_shapes=[
            pltpu.VMEM((hw, t), BF16),
            pltpu.VMEM((tq, tq), F32),
            pltpu.VMEM((tq, tq), BF16),
            pltpu.VMEM((2, hw, tq), F32),
        ],
        compiler_params=_params("parallel", "parallel", "arbitrary"),
        name="attn_prompt",
    )(slopes, lam, q, k, v, subln.reshape(1, hw))


def _attn_sample_kernel(slopes_ref, lam_ref, q_ref, kc_ref, vlo_ref, vhi_ref, kn_ref, vn_ref, g_ref,
                        o_ref, m_ref, l_ref, acc_ref, bias_ref, *, past, lam_init):
    kt = pl.program_id(1)
    nkt = pl.num_programs(1)
    nq = q_ref.shape[0]
    nh, ex = ATTN_HEADS, SAMPLE_EXPAND
    ng = nh // ex
    tk = vlo_ref.shape[0] // nh
    n = ex * tk
    rows = ex * nq
    dh = ATTN_HEAD_DIM
    hw = 2 * dh
    lam = lam_ref[0]

    def q_stack(g, c):
        return jnp.concatenate([q_ref[:, (g + ng * e) * hw + c * dh:(g + ng * e) * hw + (c + 1) * dh]
                                for e in range(ex)], axis=0)

    def scores(g, keys):
        return jnp.concatenate([lax.dot_general(q_stack(g, c), keys[c], _NT,
                                                preferred_element_type=F32) for c in range(2)], axis=0)

    def slope_rows(g):
        e_io = lax.broadcasted_iota(jnp.int32, (2 * rows, 1), 0) % rows // nq
        out = jnp.zeros((2 * rows, 1), F32)
        for e in range(ex):
            out = jnp.where(e_io == e, slopes_ref[g + ng * e] * LOG2E, out)
        return out

    def update(g, s, vb):
        m, l, acc = _softmax_step(s, m_ref[g][:, :1], l_ref[g][:, :1], acc_ref[g], vb)
        m_ref[g] = jnp.broadcast_to(m, m_ref.shape[1:])
        l_ref[g] = jnp.broadcast_to(l, l_ref.shape[1:])
        acc_ref[g] = acc

    @pl.when(kt == 0)
    def _():
        m_ref[...] = jnp.full(m_ref.shape, NEG_BIG, F32)
        l_ref[...] = jnp.zeros(l_ref.shape, F32)
        acc_ref[...] = jnp.zeros(acc_ref.shape, F32)
        e_row = lax.broadcasted_iota(jnp.int32, (2 * rows, n), 0) % rows // nq
        c_io = lax.broadcasted_iota(jnp.int32, (2 * rows, n), 1)
        for g in range(ng):
            bias_ref[g] = jnp.where(e_row == c_io % ex,
                                    slope_rows(g) * (c_io // ex - past).astype(F32), NEG_BIG)

    for g in range(ng):
        keys = [kc_ref[pl.ds(2 * g + c, n, stride=2 * ng), :].astype(BF16) for c in range(2)]
        vals = jnp.concatenate([vlo_ref[pl.ds(g, n, stride=ng), :], vhi_ref[pl.ds(g, n, stride=ng), :]],
                               axis=1).astype(BF16)
        update(g, scores(g, keys) + (bias_ref[g] + slope_rows(g) * (kt * tk).astype(F32)), vals)

    @pl.when(kt == nkt - 1)
    def _():
        r_io = lax.broadcasted_iota(jnp.int32, (2 * rows, rows), 0) % rows
        c_io = lax.broadcasted_iota(jnp.int32, (2 * rows, rows), 1)
        tok, key = r_io % nq, c_io % nq
        rel_n = (tok - jnp.abs(tok - key)).astype(F32)
        allowed = (r_io // nq == c_io // nq) & (((key + past) // CHUNK) <= ((tok + past) // CHUNK))
        for g in range(ng):
            heads = [g + ng * e for e in range(ex)]
            keys = [jnp.concatenate([kn_ref[:, h * hw + c * dh:h * hw + (c + 1) * dh] for h in heads],
                                    axis=0) for c in range(2)]
            vals = jnp.concatenate([vn_ref[:, h * hw:(h + 1) * hw] for h in heads], axis=0)
            update(g, jnp.where(allowed, scores(g, keys) + slope_rows(g) * rel_n, NEG_BIG), vals)
            on = acc_ref[g] / l_ref[g][:, :1]
            for e, h in enumerate(heads):
                o = on[e * nq:(e + 1) * nq] - lam * on[rows + e * nq:rows + (e + 1) * nq]
                o_ref[:, h * hw:(h + 1) * hw] = _subln(o, g_ref[...], lam_init).astype(o_ref.dtype)


def attn_sample(q, k_new, v_new, k_cache, v_cache, slopes, lam, subln, lam_init, tk):
    b, nq, d = q.shape
    nh, dh = ATTN_HEADS, ATTN_HEAD_DIM
    past = v_cache.shape[1] // nh
    hw = 2 * dh
    ex = SAMPLE_EXPAND
    ng = nh // ex
    assert past % tk == 0 and past > 0 and dh == LANES and nh % ex == 0
    smem = pl.BlockSpec(memory_space=pltpu.SMEM)
    return pl.pallas_call(
        functools.partial(_attn_sample_kernel, past=past, lam_init=lam_init),
        grid=(b, past // tk),
        in_specs=[
            smem, smem,
            pl.BlockSpec((None, nq, d), lambda b, k: (b, 0, 0)),
            pl.BlockSpec((None, tk * nh * 2, dh), lambda b, k: (b, k, 0)),
            pl.BlockSpec((None, tk * nh, dh), lambda b, k: (b, k, 0)),
            pl.BlockSpec((None, tk * nh, dh), lambda b, k: (b, k, 1)),
            pl.BlockSpec((None, nq, d), lambda b, k: (b, 0, 0)),
            pl.BlockSpec((None, nq, d), lambda b, k: (b, 0, 0)),
            pl.BlockSpec((1, hw), lambda b, k: (0, 0)),
        ],
        out_specs=pl.BlockSpec((None, nq, d), lambda b, k: (b, 0, 0)),
        out_shape=jax.ShapeDtypeStruct(q.shape, BF16),
        scratch_shapes=[
            pltpu.VMEM((ng, 2 * ex * nq, LANES), F32),
            pltpu.VMEM((ng, 2 * ex * nq, LANES), F32),
            pltpu.VMEM((ng, 2 * ex * nq, hw), F32),
            pltpu.VMEM((ng, 2 * ex * nq, ex * tk), F32),
        ],
        compiler_params=_params("parallel", "arbitrary"),
        name="attn_sample",
    )(slopes, lam, q, k_cache, v_cache, v_cache, k_new, v_new, subln.reshape(1, hw))


def _cumsum_rows(tri, x):
    hi = x.astype(BF16)
    r1 = x - hi.astype(F32)
    mid = r1.astype(BF16)
    lo = (r1 - mid.astype(F32)).astype(BF16)
    n = x.shape[1]
    y = jnp.dot(tri, jnp.concatenate([hi, mid, lo], axis=1), preferred_element_type=F32)
    return y[:, :n] + y[:, n:2 * n] + y[:, 2 * n:]


def _hgrn_kernel(q_ref, k_ref, i_ref, g_ref, gn_ref, s0_ref, o_ref, sout_ref,
                 s_ref, qb_ref, kb_ref, *, chunk, sub):
    tb = pl.program_id(2)
    hb, tbs, dk = q_ref.shape
    nch = tbs // chunk
    nsub = chunk // sub

    @pl.when(tb == 0)
    def _():
        s_ref[...] = s0_ref[...]
        qb_ref[...] = jnp.zeros(qb_ref.shape, BF16)
        kb_ref[...] = jnp.zeros(kb_ref.shape, BF16)

    r_io = lax.broadcasted_iota(jnp.int32, (chunk, chunk), 0)
    c_io = lax.broadcasted_iota(jnp.int32, (chunk, chunk), 1)
    causal = c_io <= r_io
    tri = causal.astype(BF16)

    for h in range(hb):
        parts = []
        for c in range(nch):
            t0, slot = c * chunk, h * nch + c
            kk = k_ref[h, t0:t0 + chunk, :]
            qs = q_ref[h, t0:t0 + chunk, :]
            v = i_ref[h, t0:t0 + chunk, :].astype(BF16)
            b = _cumsum_rows(tri, jnp.log(1.0 - kk))
            rows = [slice(sub * j, sub * (j + 1)) for j in range(nsub)]
            ends = [b[sub * (j + 1) - 1:sub * (j + 1), :] for j in range(nsub)]
            b_last = ends[-1]
            k_own = [kk[rows[j]] * jnp.exp(ends[j] - b[rows[j]]) for j in range(nsub)]
            q_end, k_end = [], []
            for i in range(nsub):
                lanes = slice(dk * i, dk * (i + 1))
                ref = ends[i - 1] if i > 0 else jnp.zeros((1, dk), F32)
                q_rel = qs[rows[i]] * jnp.exp(b[rows[i]] - ref)
                qb_ref[slot, rows[i], lanes] = q_rel.astype(BF16)
                q_end.append(q_rel * jnp.exp(ref))
                for j in range(i):
                    k_rel = k_own[j] if j == i - 1 else k_own[j] * jnp.exp(ref - ends[j])
                    kb_ref[slot, rows[j], lanes] = k_rel.astype(BF16)
                kb_ref[slot, rows[i], lanes] = (
                    kk[rows[i]] * jnp.exp(jnp.minimum(ref - b[rows[i]], EXP_CLAMP))).astype(BF16)
                k_end.append(k_own[i] if i == nsub - 1 else k_own[i] * jnp.exp(b_last - ends[i]))
            a = lax.dot_general(qb_ref[slot], kb_ref[slot], _NT, preferred_element_type=F32)
            a = jnp.where(causal, a, 0.0).astype(BF16)
            q_end = jnp.concatenate(q_end, axis=0).astype(BF16)
            k_end = jnp.concatenate(k_end, axis=0).astype(BF16)
            upd = lax.dot_general(k_end, v, _TN, preferred_element_type=F32)
            decay = jnp.transpose(jnp.broadcast_to(jnp.exp(b_last), (REC_DV, dk)))
            parts.append((jnp.concatenate([a, q_end], axis=1), v, decay, upd))
        state = s_ref[h]
        for c in range(nch):
            t0 = c * chunk
            a_q, v, decay, upd = parts[c]
            o = jnp.dot(a_q, jnp.concatenate([v, state.astype(BF16)], axis=0),
                        preferred_element_type=F32)
            state = decay * state + upd
            y = _rms(o, gn_ref[...]) * g_ref[h, t0:t0 + chunk, :]
            o_ref[t0:t0 + chunk, h * REC_DV:(h + 1) * REC_DV] = y.astype(o_ref.dtype)
        s_ref[h] = state
    sout_ref[...] = s_ref[...]


def hgrn(qfig, s0, out_norm, hb, tbs, chunk, sub):
    _, nh, b, t, dk = qfig.shape
    dv = REC_DV
    assert t % tbs == 0 and tbs % chunk == 0 and chunk % sub == 0 and sub % 16 == 0 and nh % hb == 0
    nsub = chunk // sub
    nslot = hb * (tbs // chunk)

    def kind(k):
        return pl.BlockSpec((None, hb, None, tbs, dk), lambda b, h, i: (k, h, b, i, 0))

    return pl.pallas_call(
        functools.partial(_hgrn_kernel, chunk=chunk, sub=sub),
        grid=(b, nh // hb, t // tbs),
        in_specs=[
            kind(0), kind(1), kind(2), kind(3),
            pl.BlockSpec((1, dv), lambda b, h, i: (0, 0)),
            pl.BlockSpec((None, hb, dk, dv), lambda b, h, i: (b, h, 0, 0)),
        ],
        out_specs=[
            pl.BlockSpec((None, tbs, hb * dv), lambda b, h, i: (b, i, h)),
            pl.BlockSpec((None, hb, dk, dv), lambda b, h, i: (b, h, 0, 0)),
        ],
        out_shape=[
            jax.ShapeDtypeStruct((b, t, nh * dv), BF16),
            jax.ShapeDtypeStruct((b, nh, dk, dv), F32),
        ],
        scratch_shapes=[
            pltpu.VMEM((hb, dk, dv), F32),
            pltpu.VMEM((nslot, chunk, nsub * dk), BF16),
            pltpu.VMEM((nslot, chunk, nsub * dk), BF16),
        ],
        compiler_params=_params("parallel", "parallel", "arbitrary"),
        name="hgrn",
    )(qfig, qfig, qfig, qfig, out_norm.reshape(1, dv), s0)


def _pick(n, candidates):
    for c in candidates:
        if n % c == 0:
            return c
    return n


def _trunk(x3, is_prompt, cache_k, cache_v, state_hgrn, state_conv, p, ffn_weights):
    nseq, L, d = x3.shape
    m = nseq * L
    dff = p["ffn_conv_b"].shape[1]
    tm = _pick(m, (1024, 512, 256))
    tf = _pick(dff, (512, 256, 128))
    if is_prompt:
        ffn_bs, ffn_lt = 1, _pick(L, (512, 256, 128))
    else:
        ffn_bs, ffn_lt = nseq, L
    x = x3.reshape(m, d)

    wqkv = p["attn_w_qkv"][0]
    nqk = ATTN_HEADS * 2 * ATTN_HEAD_DIM
    g0 = p["mixer_norm"][0]
    tmw = _pick(m, (512, 256))
    q, xn = norm_matmul(x, g0, wqkv, 0, nqk, ((BF16, "rows"), (BF16, "xn")), tmw, nqk,
                        out_scale=ATTN_SCALE * LOG2E)
    k32, k16 = norm_matmul(xn, None, wqkv, nqk, nqk, ((F32, "interleaved"), (BF16, "rows")), tmw, nqk)
    v32, v16 = norm_matmul(xn, None, wqkv, 2 * nqk, nqk, ((F32, "rows"), (BF16, "rows")), tmw, nqk)
    sh = (nseq, L, nqk)
    lam_init = 0.8 - 0.6 * math.exp(-0.3 * 0)
    if is_prompt:
        o = attn_prompt(q.reshape(sh), k16.reshape(sh), v16.reshape(sh), p["slopes"], p["lam"],
                        p["attn_subln"][0], lam_init, tq=_pick(L, (512, 256, 128, 64)))
    else:
        past = cache_k.shape[2]
        o = attn_sample(q.reshape(sh), k16.reshape(sh), v16.reshape(sh),
                        cache_k[0].reshape(nseq, past * ATTN_HEADS * 2, ATTN_HEAD_DIM),
                        cache_v[0].reshape(nseq, past * ATTN_HEADS, ATTN_V_DIM),
                        p["slopes"], p["lam"], p["attn_subln"][0], lam_init,
                        tk=_pick(past, (1024, 512, 256, 128)))
    x, xn = matmul_res(o.reshape(m, nqk), p["attn_w_o"][0], x, p["ffn_norm"][0], tmw)
    prev0 = jnp.zeros((nseq, CONV_W - 1, dff), F32) if is_prompt else state_conv[0]
    def ffn(i, x, xn, prev, out_g, final):
        w_gate, w_val, val_col0, w_down, layer = ffn_weights[i]
        return conv_ffn(x.reshape(nseq, L, d), xn.reshape(nseq, L, d), prev, w_gate, w_val, val_col0,
                        w_down, layer, p["ffn_conv_w"], p["ffn_conv_b"], i, out_g, final,
                        ffn_bs, ffn_lt, tf)

    x3b, xn3, cs0, *w16_0 = ffn(0, x, xn, prev0, p["mixer_norm"][1], False)
    x = x3b.reshape(m, d)

    wq = p["rec_w_qfig"][0]
    (qfig,) = norm_matmul(xn3.reshape(m, d), None, wq, 0, wq.shape[1], ((F32, "groups"),), tm, 512,
                          hgrn_lb=p["lb"])
    if is_prompt:
        s0 = jnp.zeros((nseq, REC_HEADS, REC_DK, REC_DV), F32)
        chunk = _pick(L, (128, 64, 32, 16))
        hb, tbs, sub = 2, _pick(L, (1024, 512, 256, 128, 64, 32, 16)), 16
    else:
        s0 = state_hgrn[0]
        hb, chunk, tbs, sub = REC_HEADS, L, L, 16
    o, s_fin = hgrn(qfig.reshape(4, REC_HEADS, nseq, L, REC_DK), s0, p["rec_out_norm"][0],
                    hb, tbs, chunk, sub)
    x, xn = matmul_res(o.reshape(m, d), p["rec_w_o"][0], x, p["ffn_norm"][1], tmw)
    prev1 = jnp.zeros((nseq, CONV_W - 1, dff), F32) if is_prompt else state_conv[1]
    y, cs1, *w16_1 = ffn(1, x, xn, prev1, p["final_norm"], True)
    kshape = (1, nseq, L, ATTN_HEADS, 2, ATTN_HEAD_DIM)
    vshape = (1, nseq, L, ATTN_HEADS, ATTN_V_DIM)
    outs = (y, k32.reshape(kshape), v32.reshape(vshape), s_fin[None], jnp.stack([cs0, cs1]))
    w16 = [(w[0], w[1], 0, w[2], 0) for w in (w16_0, w16_1) if w]
    return outs, w16


def kernel(x_prompt, x_sample, cache_k, cache_v, state_hgrn, state_conv, mixer_norm, ffn_norm,
           attn_w_qkv, attn_lambda_q1, attn_lambda_k1, attn_lambda_q2, attn_lambda_k2, attn_subln,
           attn_w_o, rec_w_qfig, rec_lower_bounds, rec_out_norm, rec_w_o, ffn_w_up, ffn_conv_w,
           ffn_conv_b, ffn_w_down, final_norm):
    lam_init = 0.8 - 0.6 * math.exp(-0.3 * 0)
    lam = (jnp.exp(jnp.sum(attn_lambda_q1[0] * attn_lambda_k1[0]))
           - jnp.exp(jnp.sum(attn_lambda_q2[0] * attn_lambda_k2[0])) + lam_init).reshape(1)
    slopes = jnp.exp2(-8.0 * jnp.arange(1, ATTN_HEADS + 1, dtype=F32) / ATTN_HEADS)
    lbs = jax.nn.softmax(rec_lower_bounds.astype(F32), axis=0)
    lb = (jnp.cumsum(lbs, axis=0) - lbs[0])[1]
    p = dict(
        mixer_norm=mixer_norm, ffn_norm=ffn_norm, attn_subln=attn_subln, rec_out_norm=rec_out_norm,
        ffn_conv_w=ffn_conv_w, ffn_conv_b=ffn_conv_b, final_norm=final_norm,
        lam=lam, slopes=slopes, lb=lb,
        attn_w_qkv=attn_w_qkv.astype(BF16), attn_w_o=attn_w_o.astype(BF16),
        rec_w_qfig=rec_w_qfig.astype(BF16), rec_w_o=rec_w_o.astype(BF16),
    )
    dff = ffn_w_down.shape[1]
    ffn32 = [(ffn_w_up, ffn_w_up, dff, ffn_w_down, i) for i in range(ffn_w_down.shape[0])]
    (ys, ks, vs, ss, cs), ffn16 = _trunk(x_sample, False, cache_k, cache_v, state_hgrn, state_conv,
                                         p, ffn32)
    (yp, kp, vp, sp, cp), _ = _trunk(x_prompt, True, None, None, None, None, p, ffn16)
    return (yp, ys, kp, vp, ks, vs, sp, ss, cp, cs)
```

```python
import functools
import math

import jax
import jax.numpy as jnp
from jax import lax
from jax.experimental import pallas as pl
from jax.experimental.pallas import tpu as pltpu

F32 = jnp.float32
BF16 = jnp.bfloat16

CHUNK = 64
EPS = 1e-6
ATTN_HEADS = 8
ATTN_HEAD_DIM = 128
ATTN_V_DIM = 2 * ATTN_HEAD_DIM
ATTN_SCALE = ATTN_HEAD_DIM ** -0.5
LOG2E = math.log2(math.e)
REC_HEADS = 16
REC_DK = 128
REC_DV = 128
CONV_W = 3
NEG_BIG = -1e30
EXP_CLAMP = 80.0

V7X_VMEM_LIMIT = 56 * 1024 * 1024

_NT = (((1,), (1,)), ((), ()))
_TN = (((0,), (0,)), ((), ()))


def _params(*sem):
    return pltpu.CompilerParams(dimension_semantics=sem, vmem_limit_bytes=V7X_VMEM_LIMIT)


def _rms(x, g):
    ms = jnp.mean(x * x, axis=-1, keepdims=True)
    return (x * lax.rsqrt(ms + EPS)) * g


def _silu(x):
    return x / (1.0 + jnp.exp(-x))


LANES = 128
NORM_MATMUL_CHUNK = 256
SAMPLE_EXPAND = 4


def _norm_matmul_kernel(x_ref, *rest, layouts, out_scale, gate_tiles, prenorm):
    if not prenorm:
        g_ref, rest = rest[0], rest[1:]
    w_ref, rest = rest[0], rest[1:]
    if gate_tiles:
        lb_ref, rest = rest[0], rest[1:]
    outs = rest[:len(layouts)]
    xn_ref = x_ref if prenorm else rest[len(layouts)]
    j = pl.program_id(1)

    if not prenorm:
        @pl.when(j == 0)
        def _():
            xn_ref[...] = _rms(x_ref[...], g_ref[...]).astype(BF16)
            for o, layout in zip(outs, layouts):
                if layout == "xn":
                    o[...] = xn_ref[...]

    tm, tn = xn_ref.shape[0], w_ref.shape[1]
    ngrp = tn // LANES
    cw = min(tn, NORM_MATMUL_CHUNK)

    def run(act):
        for c0 in range(0, tn, cw):
            wt = w_ref[:, c0:c0 + cw]
            if wt.dtype != BF16:
                wt = wt.astype(BF16)
                for o, layout in zip(outs, layouts):
                    if layout == "w16":
                        o[:, c0:c0 + cw] = wt
            val = jnp.dot(xn_ref[...], wt, preferred_element_type=F32)
            if out_scale is not None:
                val = val * out_scale
            val = act(val, c0)
            for o, layout in zip(outs, layouts):
                if layout in ("xn", "w16"):
                    continue
                if layout == "rows":
                    o[:, c0:c0 + cw] = val.astype(o.dtype)
                    continue
                for gi in range(cw // LANES):
                    piece = val[:, gi * LANES:(gi + 1) * LANES].astype(o.dtype)
                    if layout == "groups":
                        o[c0 // LANES + gi] = piece
                    else:
                        o[pl.ds(c0 // LANES + gi, tm, stride=ngrp), :] = piece

    if not gate_tiles:
        run(lambda val, c0: val)
    else:
        kind = j // gate_tiles

        @pl.when((kind == 0) | (kind == 3))
        def _():
            run(lambda val, c0: _silu(val))

        @pl.when(kind == 1)
        def _():
            run(lambda val, c0: (1.0 - lb_ref[:, c0:c0 + cw]) / (1.0 + jnp.exp(val)))

        @pl.when(kind == 2)
        def _():
            run(lambda val, c0: val)


def norm_matmul(x, g, w, col0, ncols, outs, tm, tn, out_scale=None, hgrn_lb=None):
    m, d = x.shape
    assert m % tm == 0 and ncols % tn == 0 and col0 % tn == 0 and tn % LANES == 0
    joff = col0 // tn
    ngrp = tn // LANES
    prenorm = g is None
    assert (x.dtype == BF16) == prenorm
    g_in = [] if prenorm else [g.reshape(1, d)]
    g_spec = [] if prenorm else [pl.BlockSpec((1, d), lambda i, j: (0, 0))]
    extra_in, extra_specs, gate_tiles = [], [], 0
    if hgrn_lb is not None:
        quarter = ncols // 4
        assert quarter % tn == 0 and col0 == 0
        gate_tiles = quarter // tn
        zeros = jnp.zeros((quarter,), F32)
        extra_in = [jnp.concatenate([zeros, hgrn_lb, zeros, zeros]).reshape(1, ncols)]
        extra_specs = [pl.BlockSpec((1, tn), lambda i, j: (0, j))]
    specs, shapes = [], []
    for dt, layout in outs:
        if layout == "xn":
            assert dt == BF16 and not prenorm
            specs.append(pl.BlockSpec((tm, d), lambda i, j: (i, 0)))
            shapes.append(jax.ShapeDtypeStruct((m, d), dt))
        elif layout == "w16":
            assert dt == BF16 and w.dtype == F32 and m == tm
            specs.append(pl.BlockSpec((d, tn), lambda i, j: (0, j)))
            shapes.append(jax.ShapeDtypeStruct((d, ncols), dt))
        elif layout == "rows":
            specs.append(pl.BlockSpec((tm, tn), lambda i, j: (i, j)))
            shapes.append(jax.ShapeDtypeStruct((m, ncols), dt))
        elif layout == "groups":
            specs.append(pl.BlockSpec((ngrp, tm, LANES), lambda i, j: (j, i, 0)))
            shapes.append(jax.ShapeDtypeStruct((ncols // LANES, m, LANES), dt))
        else:
            assert layout == "interleaved" and tn == ncols and dt == F32
            specs.append(pl.BlockSpec((tm * ngrp, LANES), lambda i, j: (i, 0)))
            shapes.append(jax.ShapeDtypeStruct((m * ngrp, LANES), dt))
    return pl.pallas_call(
        functools.partial(_norm_matmul_kernel, layouts=tuple(l for _, l in outs),
                          out_scale=out_scale, gate_tiles=gate_tiles, prenorm=prenorm),
        grid=(m // tm, ncols // tn),
        in_specs=[pl.BlockSpec((tm, d), lambda i, j: (i, 0))] + g_spec
        + [pl.BlockSpec((d, tn), lambda i, j: (0, j + joff))] + extra_specs,
        out_specs=specs,
        out_shape=shapes,
        scratch_shapes=[] if prenorm else [pltpu.VMEM((tm, d), BF16)],
        compiler_params=_params("parallel", "arbitrary"),
        name="norm_matmul",
    )(x, *g_in, w, *extra_in)


def _matmul_res_kernel(a_ref, w_ref, r_ref, g_ref, o_ref, on_ref):
    out = r_ref[...] + jnp.dot(a_ref[...], w_ref[...], preferred_element_type=F32)
    o_ref[...] = out
    on_ref[...] = _rms(out, g_ref[...]).astype(on_ref.dtype)


def matmul_res(a, w, res, g, tm):
    m, k = a.shape
    n = w.shape[1]
    assert m % tm == 0
    return pl.pallas_call(
        _matmul_res_kernel,
        grid=(m // tm,),
        in_specs=[
            pl.BlockSpec((tm, k), lambda i: (i, 0)),
            pl.BlockSpec((k, n), lambda i: (0, 0)),
            pl.BlockSpec((tm, n), lambda i: (i, 0)),
            pl.BlockSpec((1, n), lambda i: (0, 0)),
        ],
        out_specs=[pl.BlockSpec((tm, n), lambda i: (i, 0)), pl.BlockSpec((tm, n), lambda i: (i, 0))],
        out_shape=[jax.ShapeDtypeStruct((m, n), F32), jax.ShapeDtypeStruct((m, n), BF16)],
        compiler_params=_params("parallel"),
        name="matmul_res",
    )(a, w, res, g.reshape(1, n))


def _ffn_kernel(x_ref, xn_ref, prev_ref, wg_ref, wv_ref, cw_ref, cb_ref, wd_ref, og_ref, *rest,
                final, nsplit):
    rest = list(rest)
    y_ref = rest.pop(0)
    yn_ref = None if final else rest.pop(0)
    cs_ref = rest.pop(0)
    emit_w = wg_ref.dtype != BF16
    if emit_w:
        wg16_ref, wv16_ref, wd16_ref = rest.pop(0), rest.pop(0), rest.pop(0)
    ext_ref, carry_ref = rest
    l = pl.program_id(1)
    j = pl.program_id(2)
    nj = pl.num_programs(2)
    bs, lt, d = x_ref.shape
    tf = wg_ref.shape[1]
    halo = CONV_W - 1

    @pl.when(j == 0)
    def _():
        y_ref[...] = x_ref[...]

    @pl.when(l == 0)
    def _():
        ext_ref[0, :, 8 - halo:8, :] = prev_ref[...]

    @pl.when(l > 0)
    def _():
        ext_ref[0, :, 8 - halo:8, :] = carry_ref[j]

    lp = lt // nsplit
    rp = bs * lp
    last = None
    if emit_w:
        wg, wv, wd = (r[...].astype(BF16) for r in (wg_ref, wv_ref, wd_ref))
        wg16_ref[...], wv16_ref[...], wd16_ref[...] = wg, wv, wd
    for s in range(nsplit):
        xn = xn_ref[:, s * lp:(s + 1) * lp, :].reshape(rp, d)
        if not emit_w:
            wg, wv, wd = wg_ref[...], wv_ref[...], wd_ref[...]
        gate = jnp.dot(xn, wg, preferred_element_type=F32).reshape(bs, lp, tf)
        val = jnp.dot(xn, wv, preferred_element_type=F32).reshape(bs, lp, tf)
        if s > 0:
            ext_ref[s, :, 8 - halo:8, :] = last
        ext_ref[s, :, 8:8 + lp, :] = gate
        last = gate[:, lp - halo:lp, :]
        c = cb_ref[...].reshape(1, 1, tf)
        for t in range(CONV_W):
            off = 8 - halo + t
            c = c + ext_ref[s, :, off:off + lp, :] * cw_ref[t:t + 1, :].reshape(1, 1, tf)
        act = (_silu(c) * val).reshape(rp, tf).astype(BF16)
        upd = jnp.dot(act, wd, preferred_element_type=F32)
        y_ref[:, s * lp:(s + 1) * lp, :] += upd.reshape(bs, lp, d)
    carry_ref[j] = last
    cs_ref[...] = last

    @pl.when(j == nj - 1)
    def _():
        yn = _rms(y_ref[...], og_ref[...].reshape(1, 1, d))
        if final:
            y_ref[...] = yn
        else:
            yn_ref[...] = yn.astype(yn_ref.dtype)


def conv_ffn(x3, xn3, prev, w_gate, w_val, val_col0, w_down, layer, conv_w, conv_b, conv_layer,
             out_g, final, bs, lt, tf):
    nseq, L, d = x3.shape
    nlayers, dff, _ = w_down.shape
    assert nseq % bs == 0 and L % lt == 0 and dff % tf == 0 and lt % 8 == 0 and val_col0 % tf == 0
    nj = dff // tf
    voff = val_col0 // tf
    emit_w = w_down.dtype != BF16
    assert not emit_w or (nseq == bs and L == lt)
    halo = CONV_W - 1
    nsplit = 2 if (bs == 1 and lt % 256 == 0) else 1
    row_tile = pl.BlockSpec((bs, lt, d), lambda s, l, j: (s, l, 0))
    tail_tile = pl.BlockSpec((bs, None, halo, tf), lambda s, l, j: (s, l, 0, j))
    y_shape = jax.ShapeDtypeStruct((nseq, L, d), F32)
    tail_shape = jax.ShapeDtypeStruct((nseq, L // lt, halo, dff), F32)
    out_specs = [row_tile] + ([] if final else [row_tile]) + [tail_tile]
    out_shape = [y_shape] + ([] if final else [jax.ShapeDtypeStruct((nseq, L, d), BF16)]) + [tail_shape]
    if emit_w:
        out_specs += [pl.BlockSpec((None, d, tf), lambda s, l, j: (0, 0, j)),
                      pl.BlockSpec((None, d, tf), lambda s, l, j: (0, 0, j)),
                      pl.BlockSpec((None, tf, d), lambda s, l, j: (0, j, 0))]
        out_shape += [jax.ShapeDtypeStruct((1, d, dff), BF16), jax.ShapeDtypeStruct((1, d, dff), BF16),
                      jax.ShapeDtypeStruct((1, dff, d), BF16)]
    n_main = 2 if final else 3
    outs = pl.pallas_call(
        functools.partial(_ffn_kernel, final=final, nsplit=nsplit),
        grid=(nseq // bs, L // lt, nj),
        in_specs=[
            row_tile,
            row_tile,
            pl.BlockSpec((bs, halo, tf), lambda s, l, j: (s, 0, j)),
            pl.BlockSpec((None, d, tf), lambda s, l, j: (layer, 0, j)),
            pl.BlockSpec((None, d, tf), lambda s, l, j: (layer, 0, j + voff)),
            pl.BlockSpec((None, CONV_W, tf), lambda s, l, j: (conv_layer, 0, j)),
            pl.BlockSpec((None, 1, tf), lambda s, l, j: (conv_layer, 0, j)),
            pl.BlockSpec((None, tf, d), lambda s, l, j: (layer, j, 0)),
            pl.BlockSpec((1, d), lambda s, l, j: (0, 0)),
        ],
        out_specs=out_specs,
        out_shape=out_shape,
        scratch_shapes=[
            pltpu.VMEM((nsplit, bs, lt // nsplit + 8, tf), F32),
            pltpu.VMEM((nj, bs, halo, tf), F32),
        ],
        compiler_params=_params("parallel", "arbitrary", "arbitrary"),
        name="conv_ffn",
    )(x3, xn3, prev, w_gate, w_val, conv_w, conv_b.reshape(-1, 1, dff), w_down,
      out_g.reshape(1, d))
    return tuple(outs[:n_main - 1]) + (outs[n_main - 1][:, -1],) + tuple(outs[n_main:])


def _softmax_step(s, m, l, acc, vb):
    m_new = jnp.maximum(m, jnp.max(s, axis=-1, keepdims=True))
    alpha = jnp.exp2(m - m_new)
    p = jnp.exp2(s - m_new)
    l_new = alpha * l + jnp.sum(p, axis=-1, keepdims=True)
    acc_new = alpha * acc + jnp.dot(p.astype(BF16), vb, preferred_element_type=F32)
    return m_new, l_new, acc_new


def _subln(o, g, lam_init):
    return _rms(o, g) * (1.0 - lam_init)


def _attn_prompt_kernel(slopes_ref, lam_ref, q_ref, k_ref, v_ref, g_ref, o_ref,
                        vt_ref, own_ref, p_ref, acc_ref, *, lam_init):
    h = pl.program_id(1)
    qi = pl.program_id(2)
    t = q_ref.shape[0]
    dh = ATTN_HEAD_DIM
    slope2 = slopes_ref[h] * LOG2E
    lam = lam_ref[0]
    q0 = qi * t

    @pl.when(qi == 0)
    def _():
        vt_ref[...] = v_ref[...].T
        k_io = lax.broadcasted_iota(jnp.int32, (t, t), 0)
        q_io = lax.broadcasted_iota(jnp.int32, (t, t), 1)
        own_ref[...] = jnp.where((k_io // CHUNK) <= (q_io // CHUNK),
                                 slope2 * (q_io - jnp.abs(q_io - k_io)).astype(F32), NEG_BIG)

    q = q_ref[...]

    def probs(c, kb, bias, m, l):
        s = lax.dot_general(kb[:, c * dh:(c + 1) * dh], q[:, c * dh:(c + 1) * dh], _NT,
                            preferred_element_type=F32) + bias
        m_new = jnp.maximum(m, jnp.max(s, axis=0, keepdims=True))
        alpha = jnp.exp2(m - m_new)
        p = jnp.exp2(s - m_new)
        return m_new, alpha * l + jnp.sum(p, axis=0, keepdims=True), alpha, p.astype(BF16)

    def add_values(c, k0, alpha, p):
        acc_ref[c] = alpha * acc_ref[c] + jnp.dot(vt_ref[:, pl.ds(k0, t)], p,
                                                  preferred_element_type=F32)

    def tile(k0, prev0, bias, state):
        (m0, l0), (m1, l1, alpha1) = state
        kb = k_ref[pl.ds(k0, t), :]
        m0, l0, alpha0, p0 = probs(0, kb, bias, m0, l0)
        if prev0 is not None:
            add_values(1, prev0, alpha1, p_ref[...])
        m1, l1, alpha1, p1 = probs(1, kb, bias, m1, l1)
        p_ref[...] = p1
        add_values(0, k0, alpha0, p0)
        return (m0, l0), (m1, l1, alpha1)

    neg, zero = jnp.full((1, t), NEG_BIG, F32), jnp.zeros((1, t), F32)
    acc_ref[...] = jnp.zeros(acc_ref.shape, F32)
    own0 = pl.multiple_of(q0, t)
    state = tile(own0, None, own_ref[...], ((neg, zero), (neg, zero, zero)))

    key = lax.broadcasted_iota(jnp.int32, (t, LANES), 0)

    def past(kt, state):
        k0 = pl.multiple_of(kt * t, t)
        prev0 = pl.multiple_of(jnp.where(kt == 0, q0, k0 - t), t)
        bias = slope2 * (key + (k0 - q0)).astype(F32)
        return tile(k0, prev0, jnp.concatenate([bias] * (t // LANES), axis=1), state)

    (_, l0), (_, l1, alpha1) = lax.fori_loop(0, qi, past, state)
    add_values(1, pl.multiple_of(jnp.where(qi == 0, q0, q0 - t), t), alpha1, p_ref[...])
    a0, a1 = acc_ref[0], acc_ref[1]
    o = (a0 / l0 - lam * (a1 / l1)).T
    o_ref[...] = _subln(o, g_ref[...], lam_init).astype(o_ref.dtype)


def attn_prompt(q, k, v, slopes, lam, subln, lam_init, tq):
    b, t, _ = q.shape
    hw = 2 * ATTN_HEAD_DIM
    assert t % tq == 0 and tq % CHUNK == 0
    smem = pl.BlockSpec(memory_space=pltpu.SMEM)
    return pl.pallas_call(
        functools.partial(_attn_prompt_kernel, lam_init=lam_init),
        grid=(b, ATTN_HEADS, t // tq),
        in_specs=[
            smem, smem,
            pl.BlockSpec((None, tq, hw), lambda b, h, i: (b, i, h)),
            pl.BlockSpec((None, t, hw), lambda b, h, i: (b, 0, h)),
            pl.BlockSpec((None, t, hw), lambda b, h, i: (b, 0, h)),
            pl.BlockSpec((1, hw), lambda b, h, i: (0, 0)),
        ],
        out_specs=pl.BlockSpec((None, tq, hw), lambda b, h, i: (b, i, h)),
        out_shape=jax.ShapeDtypeStruct(q.shape, BF16),
        scratch_shapes=[
            pltpu.VMEM((hw, t), BF16),
            pltpu.VMEM((tq, tq), F32),
            pltpu.VMEM((tq, tq), BF16),
            pltpu.VMEM((2, hw, tq), F32),
        ],
        compiler_params=_params("parallel", "parallel", "arbitrary"),
        name="attn_prompt",
    )(slopes, lam, q, k, v, subln.reshape(1, hw))


def _attn_sample_kernel(slopes_ref, lam_ref, q_ref, kc_ref, vlo_ref, vhi_ref, kn_ref, vn_ref, g_ref,
                        o_ref, m_ref, l_ref, acc_ref, bias_ref, *, past, lam_init):
    kt = pl.program_id(1)
    nkt = pl.num_programs(1)
    nq = q_ref.shape[0]
    nh, ex = ATTN_HEADS, SAMPLE_EXPAND
    ng = nh // ex
    tk = vlo_ref.shape[0] // nh
    n = ex * tk
    rows = ex * nq
    dh = ATTN_HEAD_DIM
    hw = 2 * dh
    lam = lam_ref[0]

    def q_stack(g, c):
        return jnp.concatenate([q_ref[:, (g + ng * e) * hw + c * dh:(g + ng * e) * hw + (c + 1) * dh]
                                for e in range(ex)], axis=0)

    def scores(g, keys):
        return jnp.concatenate([lax.dot_general(q_stack(g, c), keys[c], _NT,
                                                preferred_element_type=F32) for c in range(2)], axis=0)

    def slope_rows(g):
        e_io = lax.broadcasted_iota(jnp.int32, (2 * rows, 1), 0) % rows // nq
        out = jnp.zeros((2 * rows, 1), F32)
        for e in range(ex):
            out = jnp.where(e_io == e, slopes_ref[g + ng * e] * LOG2E, out)
        return out

    def update(g, s, vb):
        m, l, acc = _softmax_step(s, m_ref[g][:, :1], l_ref[g][:, :1], acc_ref[g], vb)
        m_ref[g] = jnp.broadcast_to(m, m_ref.shape[1:])
        l_ref[g] = jnp.broadcast_to(l, l_ref.shape[1:])
        acc_ref[g] = acc

    @pl.when(kt == 0)
    def _():
        m_ref[...] = jnp.full(m_ref.shape, NEG_BIG, F32)
        l_ref[...] = jnp.zeros(l_ref.shape, F32)
        acc_ref[...] = jnp.zeros(acc_ref.shape, F32)
        e_row = lax.broadcasted_iota(jnp.int32, (2 * rows, n), 0) % rows // nq
        c_io = lax.broadcasted_iota(jnp.int32, (2 * rows, n), 1)
        for g in range(ng):
            bias_ref[g] = jnp.where(e_row == c_io % ex,
                                    slope_rows(g) * (c_io // ex - past).astype(F32), NEG_BIG)

    for g in range(ng):
        keys = [kc_ref[pl.ds(2 * g + c, n, stride=2 * ng), :].astype(BF16) for c in range(2)]
        vals = jnp.concatenate([vlo_ref[pl.ds(g, n, stride=ng), :], vhi_ref[pl.ds(g, n, stride=ng), :]],
                               axis=1).astype(BF16)
        update(g, scores(g, keys) + (bias_ref[g] + slope_rows(g) * (kt * tk).astype(F32)), vals)

    @pl.when(kt == nkt - 1)
    def _():
        r_io = lax.broadcasted_iota(jnp.int32, (2 * rows, rows), 0) % rows
        c_io = lax.broadcasted_iota(jnp.int32, (2 * rows, rows), 1)
        tok, key = r_io % nq, c_io % nq
        rel_n = (tok - jnp.abs(tok - key)).astype(F32)
        allowed = (r_io // nq == c_io // nq) & (((key + past) // CHUNK) <= ((tok + past) // CHUNK))
        for g in range(ng):
            heads = [g + ng * e for e in range(ex)]
            keys = [jnp.concatenate([kn_ref[:, h * hw + c * dh:h * hw + (c + 1) * dh] for h in heads],
                                    axis=0) for c in range(2)]
            vals = jnp.concatenate([vn_ref[:, h * hw:(h + 1) * hw] for h in heads], axis=0)
            update(g, jnp.where(allowed, scores(g, keys) + slope_rows(g) * rel_n, NEG_BIG), vals)
            on = acc_ref[g] / l_ref[g][:, :1]
            for e, h in enumerate(heads):
                o = on[e * nq:(e + 1) * nq] - lam * on[rows + e * nq:rows + (e + 1) * nq]
                o_ref[:, h * hw:(h + 1) * hw] = _subln(o, g_ref[...], lam_init).astype(o_ref.dtype)


def attn_sample(q, k_new, v_new, k_cache, v_cache, slopes, lam, subln, lam_init, tk):
    b, nq, d = q.shape
    nh, dh = ATTN_HEADS, ATTN_HEAD_DIM
    past = v_cache.shape[1] // nh
    hw = 2 * dh
    ex = SAMPLE_EXPAND
    ng = nh // ex
    assert past % tk == 0 and past > 0 and dh == LANES and nh % ex == 0
    smem = pl.BlockSpec(memory_space=pltpu.SMEM)
    return pl.pallas_call(
        functools.partial(_attn_sample_kernel, past=past, lam_init=lam_init),
        grid=(b, past // tk),
        in_specs=[
            smem, smem,
            pl.BlockSpec((None, nq, d), lambda b, k: (b, 0, 0)),
            pl.BlockSpec((None, tk * nh * 2, dh), lambda b, k: (b, k, 0)),
            pl.BlockSpec((None, tk * nh, dh), lambda b, k: (b, k, 0)),
            pl.BlockSpec((None, tk * nh, dh), lambda b, k: (b, k, 1)),
            pl.BlockSpec((None, nq, d), lambda b, k: (b, 0, 0)),
            pl.BlockSpec((None, nq, d), lambda b, k: (b, 0, 0)),
            pl.BlockSpec((1, hw), lambda b, k: (0, 0)),
        ],
        out_specs=pl.BlockSpec((None, nq, d), lambda b, k: (b, 0, 0)),
        out_shape=jax.ShapeDtypeStruct(q.shape, BF16),
        scratch_shapes=[
            pltpu.VMEM((ng, 2 * ex * nq, LANES), F32),
            pltpu.VMEM((ng, 2 * ex * nq, LANES), F32),
            pltpu.VMEM((ng, 2 * ex * nq, hw), F32),
            pltpu.VMEM((ng, 2 * ex * nq, ex * tk), F32),
        ],
        compiler_params=_params("parallel", "arbitrary"),
        name="attn_sample",
    )(slopes, lam, q, k_cache, v_cache, v_cache, k_new, v_new, subln.reshape(1, hw))


def _cumsum_rows(tri, x):
    hi = x.astype(BF16)
    r1 = x - hi.astype(F32)
    mid = r1.astype(BF16)
    lo = (r1 - mid.astype(F32)).astype(BF16)
    n = x.shape[1]
    y = jnp.dot(tri, jnp.concatenate([hi, mid, lo], axis=1), preferred_element_type=F32)
    return y[:, :n] + y[:, n:2 * n] + y[:, 2 * n:]


def _hgrn_kernel(q_ref, k_ref, i_ref, g_ref, gn_ref, s0_ref, o_ref, sout_ref,
                 s_ref, qb_ref, kb_ref, *, chunk, sub):
    tb = pl.program_id(2)
    hb, tbs, dk = q_ref.shape
    nch = tbs // chunk
    nsub = chunk // sub

    @pl.when(tb == 0)
    def _():
        s_ref[...] = s0_ref[...]
        qb_ref[...] = jnp.zeros(qb_ref.shape, BF16)
        kb_ref[...] = jnp.zeros(kb_ref.shape, BF16)

    r_io = lax.broadcasted_iota(jnp.int32, (chunk, chunk), 0)
    c_io = lax.broadcasted_iota(jnp.int32, (chunk, chunk), 1)
    causal = c_io <= r_io
    tri = causal.astype(BF16)

    for h in range(hb):
        parts = []
        for c in range(nch):
            t0, slot = c * chunk, h * nch + c
            kk = k_ref[h, t0:t0 + chunk, :]
            qs = q_ref[h, t0:t0 + chunk, :]
            v = i_ref[h, t0:t0 + chunk, :].astype(BF16)
            b = _cumsum_rows(tri, jnp.log(1.0 - kk))
            rows = [slice(sub * j, sub * (j + 1)) for j in range(nsub)]
            ends = [b[sub * (j + 1) - 1:sub * (j + 1), :] for j in range(nsub)]
            b_last = ends[-1]
            k_own = [kk[rows[j]] * jnp.exp(ends[j] - b[rows[j]]) for j in range(nsub)]
            q_end, k_end = [], []
            for i in range(nsub):
                lanes = slice(dk * i, dk * (i + 1))
                ref = ends[i - 1] if i > 0 else jnp.zeros((1, dk), F32)
                q_rel = qs[rows[i]] * jnp.exp(b[rows[i]] - ref)
                qb_ref[slot, rows[i], lanes] = q_rel.astype(BF16)
                q_end.append(q_rel * jnp.exp(ref))
                for j in range(i):
                    k_rel = k_own[j] if j == i - 1 else k_own[j] * jnp.exp(ref - ends[j])
                    kb_ref[slot, rows[j], lanes] = k_rel.astype(BF16)
                kb_ref[slot, rows[i], lanes] = (
                    kk[rows[i]] * jnp.exp(jnp.minimum(ref - b[rows[i]], EXP_CLAMP))).astype(BF16)
                k_end.append(k_own[i] if i == nsub - 1 else k_own[i] * jnp.exp(b_last - ends[i]))
            a = lax.dot_general(qb_ref[slot], kb_ref[slot], _NT, preferred_element_type=F32)
            a = jnp.where(causal, a, 0.0).astype(BF16)
            q_end = jnp.concatenate(q_end, axis=0).astype(BF16)
            k_end = jnp.concatenate(k_end, axis=0).astype(BF16)
            upd = lax.dot_general(k_end, v, _TN, preferred_element_type=F32)
            decay = jnp.transpose(jnp.broadcast_to(jnp.exp(b_last), (REC_DV, dk)))
            parts.append((jnp.concatenate([a, q_end], axis=1), v, decay, upd))
        state = s_ref[h]
        for c in range(nch):
            t0 = c * chunk
            a_q, v, decay, upd = parts[c]
            o = jnp.dot(a_q, jnp.concatenate([v, state.astype(BF16)], axis=0),
                        preferred_element_type=F32)
            state = decay * state + upd
            y = _rms(o, gn_ref[...]) * g_ref[h, t0:t0 + chunk, :]
            o_ref[t0:t0 + chunk, h * REC_DV:(h + 1) * REC_DV] = y.astype(o_ref.dtype)
        s_ref[h] = state
    sout_ref[...] = s_ref[...]


def hgrn(qfig, s0, out_norm, hb, tbs, chunk, sub):
    _, nh, b, t, dk = qfig.shape
    dv = REC_DV
    assert t % tbs == 0 and tbs % chunk == 0 and chunk % sub == 0 and sub % 16 == 0 and nh % hb == 0
    nsub = chunk // sub
    nslot = hb * (tbs // chunk)

    def kind(k):
        return pl.BlockSpec((None, hb, None, tbs, dk), lambda b, h, i: (k, h, b, i, 0))

    return pl.pallas_call(
        functools.partial(_hgrn_kernel, chunk=chunk, sub=sub),
        grid=(b, nh // hb, t // tbs),
        in_specs=[
            kind(0), kind(1), kind(2), kind(3),
            pl.BlockSpec((1, dv), lambda b, h, i: (0, 0)),
            pl.BlockSpec((None, hb, dk, dv), lambda b, h, i: (b, h, 0, 0)),
        ],
        out_specs=[
            pl.BlockSpec((None, tbs, hb * dv), lambda b, h, i: (b, i, h)),
            pl.BlockSpec((None, hb, dk, dv), lambda b, h, i: (b, h, 0, 0)),
        ],
        out_shape=[
            jax.ShapeDtypeStruct((b, t, nh * dv), BF16),
            jax.ShapeDtypeStruct((b, nh, dk, dv), F32),
        ],
        scratch_shapes=[
            pltpu.VMEM((hb, dk, dv), F32),
            pltpu.VMEM((nslot, chunk, nsub * dk), BF16),
            pltpu.VMEM((nslot, chunk, nsub * dk), BF16),
        ],
        compiler_params=_params("parallel", "parallel", "arbitrary"),
        name="hgrn",
    )(qfig, qfig, qfig, qfig, out_norm.reshape(1, dv), s0)


def _pick(n, candidates):
    for c in candidates:
        if n % c == 0:
            return c
    return n


def _trunk(x3, is_prompt, cache_k, cache_v, state_hgrn, state_conv, p, ffn_weights, w_qfig):
    nseq, L, d = x3.shape
    m = nseq * L
    dff = p["ffn_conv_b"].shape[1]
    tm = _pick(m, (1024, 512, 256))
    tf = _pick(dff, (512, 256, 128))
    if is_prompt:
        ffn_bs, ffn_lt = 1, _pick(L, (512, 256, 128))
    else:
        ffn_bs, ffn_lt = nseq, L
    x = x3.reshape(m, d)

    wqkv = p["attn_w_qkv"][0]
    nqk = ATTN_HEADS * 2 * ATTN_HEAD_DIM
    g0 = p["mixer_norm"][0]
    tmw = _pick(m, (512, 256))
    q, xn = norm_matmul(x, g0, wqkv, 0, nqk, ((BF16, "rows"), (BF16, "xn")), tmw, nqk,
                        out_scale=ATTN_SCALE * LOG2E)
    k32, k16 = norm_matmul(xn, None, wqkv, nqk, nqk, ((F32, "interleaved"), (BF16, "rows")), tmw, nqk)
    v32, v16 = norm_matmul(xn, None, wqkv, 2 * nqk, nqk, ((F32, "rows"), (BF16, "rows")), tmw, nqk)
    sh = (nseq, L, nqk)
    lam_init = 0.8 - 0.6 * math.exp(-0.3 * 0)
    if is_prompt:
        o = attn_prompt(q.reshape(sh), k16.reshape(sh), v16.reshape(sh), p["slopes"], p["lam"],
                        p["attn_subln"][0], lam_init, tq=_pick(L, (512, 256, 128, 64)))
    else:
        past = cache_k.shape[2]
        o = attn_sample(q.reshape(sh), k16.reshape(sh), v16.reshape(sh),
                        cache_k[0].reshape(nseq, past * ATTN_HEADS * 2, ATTN_HEAD_DIM),
                        cache_v[0].reshape(nseq, past * ATTN_HEADS, ATTN_V_DIM),
                        p["slopes"], p["lam"], p["attn_subln"][0], lam_init,
                        tk=_pick(past, (1024, 512, 256, 128)))
    x, xn = matmul_res(o.reshape(m, nqk), p["attn_w_o"][0], x, p["ffn_norm"][0], tmw)
    prev0 = jnp.zeros((nseq, CONV_W - 1, dff), F32) if is_prompt else state_conv[0]
    def ffn(i, x, xn, prev, out_g, final):
        w_gate, w_val, val_col0, w_down, layer = ffn_weights[i]
        return conv_ffn(x.reshape(nseq, L, d), xn.reshape(nseq, L, d), prev, w_gate, w_val, val_col0,
                        w_down, layer, p["ffn_conv_w"], p["ffn_conv_b"], i, out_g, final,
                        ffn_bs, ffn_lt, tf)

    x3b, xn3, cs0, *w16_0 = ffn(0, x, xn, prev0, p["mixer_norm"][1], False)
    x = x3b.reshape(m, d)

    wq = w_qfig
    wq_f32 = wq.dtype == F32
    qfig, *wq16 = norm_matmul(xn3.reshape(m, d), None, wq, 0, wq.shape[1],
                              ((F32, "groups"),) + (((BF16, "w16"),) if wq_f32 else ()),
                              tm, 512 if wq_f32 else 1024, hgrn_lb=p["lb"])
    if is_prompt:
        s0 = jnp.zeros((nseq, REC_HEADS, REC_DK, REC_DV), F32)
        chunk = _pick(L, (128, 64, 32, 16))
        hb, tbs, sub = 2, _pick(L, (1024, 512, 256, 128, 64, 32, 16)), 16
    else:
        s0 = state_hgrn[0]
        hb, chunk, tbs, sub = REC_HEADS, L, L, 16
    o, s_fin = hgrn(qfig.reshape(4, REC_HEADS, nseq, L, REC_DK), s0, p["rec_out_norm"][0],
                    hb, tbs, chunk, sub)
    x, xn = matmul_res(o.reshape(m, d), p["rec_w_o"][0], x, p["ffn_norm"][1], tmw)
    prev1 = jnp.zeros((nseq, CONV_W - 1, dff), F32) if is_prompt else state_conv[1]
    y, cs1, *w16_1 = ffn(1, x, xn, prev1, p["final_norm"], True)
    kshape = (1, nseq, L, ATTN_HEADS, 2, ATTN_HEAD_DIM)
    vshape = (1, nseq, L, ATTN_HEADS, ATTN_V_DIM)
    outs = (y, k32.reshape(kshape), v32.reshape(vshape), s_fin[None], jnp.stack([cs0, cs1]))
    w16 = [(w[0], w[1], 0, w[2], 0) for w in (w16_0, w16_1) if w]
    return outs, w16, (wq16[0] if wq16 else None)


def kernel(x_prompt, x_sample, cache_k, cache_v, state_hgrn, state_conv, mixer_norm, ffn_norm,
           attn_w_qkv, attn_lambda_q1, attn_lambda_k1, attn_lambda_q2, attn_lambda_k2, attn_subln,
           attn_w_o, rec_w_qfig, rec_lower_bounds, rec_out_norm, rec_w_o, ffn_w_up, ffn_conv_w,
           ffn_conv_b, ffn_w_down, final_norm):
    lam_init = 0.8 - 0.6 * math.exp(-0.3 * 0)
    lam = (jnp.exp(jnp.sum(attn_lambda_q1[0] * attn_lambda_k1[0]))
           - jnp.exp(jnp.sum(attn_lambda_q2[0] * attn_lambda_k2[0])) + lam_init).reshape(1)
    slopes = jnp.exp2(-8.0 * jnp.arange(1, ATTN_HEADS + 1, dtype=F32) / ATTN_HEADS)
    lbs = jax.nn.softmax(rec_lower_bounds.astype(F32), axis=0)
    lb = (jnp.cumsum(lbs, axis=0) - lbs[0])[1]
    p = dict(
        mixer_norm=mixer_norm, ffn_norm=ffn_norm, attn_subln=attn_subln, rec_out_norm=rec_out_norm,
        ffn_conv_w=ffn_conv_w, ffn_conv_b=ffn_conv_b, final_norm=final_norm,
        lam=lam, slopes=slopes, lb=lb,
        attn_w_qkv=attn_w_qkv.astype(BF16), attn_w_o=attn_w_o.astype(BF16),
        rec_w_o=rec_w_o.astype(BF16),
    )
    dff = ffn_w_down.shape[1]
    ffn32 = [(ffn_w_up, ffn_w_up, dff, ffn_w_down, i) for i in range(ffn_w_down.shape[0])]
    (ys, ks, vs, ss, cs), ffn16, wq16 = _trunk(x_sample, False, cache_k, cache_v, state_hgrn,
                                               state_conv, p, ffn32, rec_w_qfig[0])
    (yp, kp, vp, sp, cp), _, _ = _trunk(x_prompt, True, None, None, None, None, p, ffn16, wq16)
    return (yp, ys, kp, vp, ks, vs, sp, ss, cp, cs)
```

```python
import functools
import math

import jax
import jax.numpy as jnp
from jax import lax
from jax.experimental import pallas as pl
from jax.experimental.pallas import tpu as pltpu

F32 = jnp.float32
BF16 = jnp.bfloat16

CHUNK = 64
EPS = 1e-6
ATTN_HEADS = 8
ATTN_HEAD_DIM = 128
ATTN_V_DIM = 2 * ATTN_HEAD_DIM
ATTN_SCALE = ATTN_HEAD_DIM ** -0.5
LOG2E = math.log2(math.e)
REC_HEADS = 16
REC_DK = 128
REC_DV = 128
CONV_W = 3
NEG_BIG = -1e30
EXP_CLAMP = 80.0

V7X_VMEM_LIMIT = 56 * 1024 * 1024

_NT = (((1,), (1,)), ((), ()))
_TN = (((0,), (0,)), ((), ()))


def _params(*sem):
    return pltpu.CompilerParams(dimension_semantics=sem, vmem_limit_bytes=V7X_VMEM_LIMIT)


def _rms(x, g):
    ms = jnp.mean(x * x, axis=-1, keepdims=True)
    return (x * lax.rsqrt(ms + EPS)) * g


def _silu(x):
    return x / (1.0 + jnp.exp(-x))


LANES = 128
NORM_MATMUL_CHUNK = 256
SAMPLE_EXPAND = 4


def _norm_matmul_kernel(x_ref, *rest, layouts, out_scale, gate_tiles, prenorm):
    if not prenorm:
        g_ref, rest = rest[0], rest[1:]
    w_ref, rest = rest[0], rest[1:]
    if gate_tiles:
        lb_ref, rest = rest[0], rest[1:]
    outs = rest[:len(layouts)]
    xn_ref = x_ref if prenorm else rest[len(layouts)]
    j = pl.program_id(1)

    if not prenorm:
        @pl.when(j == 0)
        def _():
            xn_ref[...] = _rms(x_ref[...], g_ref[...]).astype(BF16)
            for o, layout in zip(outs, layouts):
                if layout == "xn":
                    o[...] = xn_ref[...]

    tm, tn = xn_ref.shape[0], w_ref.shape[1]
    ngrp = tn // LANES
    cw = min(tn, NORM_MATMUL_CHUNK)

    def run(act):
        for c0 in range(0, tn, cw):
            wt = w_ref[:, c0:c0 + cw]
            if wt.dtype != BF16:
                wt = wt.astype(BF16)
                for o, layout in zip(outs, layouts):
                    if layout == "w16":
                        o[:, c0:c0 + cw] = wt
            val = jnp.dot(xn_ref[...], wt, preferred_element_type=F32)
            if out_scale is not None:
                val = val * out_scale
            val = act(val, c0)
            for o, layout in zip(outs, layouts):
                if layout in ("xn", "w16"):
                    continue
                if layout == "rows":
                    o[:, c0:c0 + cw] = val.astype(o.dtype)
                    continue
                for gi in range(cw // LANES):
                    piece = val[:, gi * LANES:(gi + 1) * LANES].astype(o.dtype)
                    if layout == "groups":
                        o[c0 // LANES + gi] = piece
                    else:
                        o[pl.ds(c0 // LANES + gi, tm, stride=ngrp), :] = piece

    if not gate_tiles:
        run(lambda val, c0: val)
    else:
        kind = j // gate_tiles

        @pl.when((kind == 0) | (kind == 3))
        def _():
            run(lambda val, c0: _silu(val))

        @pl.when(kind == 1)
        def _():
            run(lambda val, c0: (1.0 - lb_ref[:, c0:c0 + cw]) / (1.0 + jnp.exp(val)))

        @pl.when(kind == 2)
        def _():
            run(lambda val, c0: val)


def norm_matmul(x, g, w, col0, ncols, outs, tm, tn, out_scale=None, hgrn_lb=None):
    m, d = x.shape
    assert m % tm == 0 and ncols % tn == 0 and col0 % tn == 0 and tn % LANES == 0
    joff = col0 // tn
    ngrp = tn // LANES
    prenorm = g is None
    assert (x.dtype == BF16) == prenorm
    g_in = [] if prenorm else [g.reshape(1, d)]
    g_spec = [] if prenorm else [pl.BlockSpec((1, d), lambda i, j: (0, 0))]
    extra_in, extra_specs, gate_tiles = [], [], 0
    if hgrn_lb is not None:
        quarter = ncols // 4
        assert quarter % tn == 0 and col0 == 0
        gate_tiles = quarter // tn
        zeros = jnp.zeros((quarter,), F32)
        extra_in = [jnp.concatenate([zeros, hgrn_lb, zeros, zeros]).reshape(1, ncols)]
        extra_specs = [pl.BlockSpec((1, tn), lambda i, j: (0, j))]
    specs, shapes = [], []
    for dt, layout in outs:
        if layout == "xn":
            assert dt == BF16 and not prenorm
            specs.append(pl.BlockSpec((tm, d), lambda i, j: (i, 0)))
            shapes.append(jax.ShapeDtypeStruct((m, d), dt))
        elif layout == "w16":
            assert dt == BF16 and w.dtype == F32 and m == tm
            specs.append(pl.BlockSpec((d, tn), lambda i, j: (0, j)))
            shapes.append(jax.ShapeDtypeStruct((d, ncols), dt))
        elif layout == "rows":
            specs.append(pl.BlockSpec((tm, tn), lambda i, j: (i, j)))
            shapes.append(jax.ShapeDtypeStruct((m, ncols), dt))
        elif layout == "groups":
            specs.append(pl.BlockSpec((ngrp, tm, LANES), lambda i, j: (j, i, 0)))
            shapes.append(jax.ShapeDtypeStruct((ncols // LANES, m, LANES), dt))
        else:
            assert layout == "interleaved" and tn == ncols and dt == F32
            specs.append(pl.BlockSpec((tm * ngrp, LANES), lambda i, j: (i, 0)))
            shapes.append(jax.ShapeDtypeStruct((m * ngrp, LANES), dt))
    return pl.pallas_call(
        functools.partial(_norm_matmul_kernel, layouts=tuple(l for _, l in outs),
                          out_scale=out_scale, gate_tiles=gate_tiles, prenorm=prenorm),
        grid=(m // tm, ncols // tn),
        in_specs=[pl.BlockSpec((tm, d), lambda i, j: (i, 0))] + g_spec
        + [pl.BlockSpec((d, tn), lambda i, j: (0, j + joff))] + extra_specs,
        out_specs=specs,
        out_shape=shapes,
        scratch_shapes=[] if prenorm else [pltpu.VMEM((tm, d), BF16)],
        compiler_params=_params("parallel", "arbitrary"),
        name="norm_matmul",
    )(x, *g_in, w, *extra_in)


def _matmul_res_kernel(a_ref, w_ref, r_ref, g_ref, o_ref, on_ref):
    out = r_ref[...] + jnp.dot(a_ref[...], w_ref[...], preferred_element_type=F32)
    o_ref[...] = out
    on_ref[...] = _rms(out, g_ref[...]).astype(on_ref.dtype)


def matmul_res(a, w, res, g, tm):
    m, k = a.shape
    n = w.shape[1]
    assert m % tm == 0
    return pl.pallas_call(
        _matmul_res_kernel,
        grid=(m // tm,),
        in_specs=[
            pl.BlockSpec((tm, k), lambda i: (i, 0)),
            pl.BlockSpec((k, n), lambda i: (0, 0)),
            pl.BlockSpec((tm, n), lambda i: (i, 0)),
            pl.BlockSpec((1, n), lambda i: (0, 0)),
        ],
        out_specs=[pl.BlockSpec((tm, n), lambda i: (i, 0)), pl.BlockSpec((tm, n), lambda i: (i, 0))],
        out_shape=[jax.ShapeDtypeStruct((m, n), F32), jax.ShapeDtypeStruct((m, n), BF16)],
        compiler_params=_params("parallel"),
        name="matmul_res",
    )(a, w, res, g.reshape(1, n))


def _ffn_kernel(x_ref, xn_ref, prev_ref, wg_ref, wv_ref, cw_ref, cb_ref, wd_ref, og_ref, *rest,
                final, nsplit):
    rest = list(rest)
    y_ref = rest.pop(0)
    yn_ref = None if final else rest.pop(0)
    cs_ref = rest.pop(0)
    emit_w = wg_ref.dtype != BF16
    if emit_w:
        wg16_ref, wv16_ref, wd16_ref = rest.pop(0), rest.pop(0), rest.pop(0)
    ext_ref, carry_ref = rest
    l = pl.program_id(1)
    j = pl.program_id(2)
    nj = pl.num_programs(2)
    bs, lt, d = x_ref.shape
    tf = wg_ref.shape[1]
    halo = CONV_W - 1

    @pl.when(j == 0)
    def _():
        y_ref[...] = x_ref[...]

    @pl.when(l == 0)
    def _():
        ext_ref[0, :, 8 - halo:8, :] = prev_ref[...]

    @pl.when(l > 0)
    def _():
        ext_ref[0, :, 8 - halo:8, :] = carry_ref[j]

    lp = lt // nsplit
    rp = bs * lp
    last = None
    if emit_w:
        wg, wv, wd = (r[...].astype(BF16) for r in (wg_ref, wv_ref, wd_ref))
        wg16_ref[...], wv16_ref[...], wd16_ref[...] = wg, wv, wd
    for s in range(nsplit):
        xn = xn_ref[:, s * lp:(s + 1) * lp, :].reshape(rp, d)
        if not emit_w:
            wg, wv, wd = wg_ref[...], wv_ref[...], wd_ref[...]
        gate = jnp.dot(xn, wg, preferred_element_type=F32).reshape(bs, lp, tf)
        val = jnp.dot(xn, wv, preferred_element_type=F32).reshape(bs, lp, tf)
        if s > 0:
            ext_ref[s, :, 8 - halo:8, :] = last
        ext_ref[s, :, 8:8 + lp, :] = gate
        last = gate[:, lp - halo:lp, :]
        c = cb_ref[...].reshape(1, 1, tf)
        for t in range(CONV_W):
            off = 8 - halo + t
            c = c + ext_ref[s, :, off:off + lp, :] * cw_ref[t:t + 1, :].reshape(1, 1, tf)
        act = (_silu(c) * val).reshape(rp, tf).astype(BF16)
        upd = jnp.dot(act, wd, preferred_element_type=F32)
        y_ref[:, s * lp:(s + 1) * lp, :] += upd.reshape(bs, lp, d)
    carry_ref[j] = last
    cs_ref[...] = last

    @pl.when(j == nj - 1)
    def _():
        yn = _rms(y_ref[...], og_ref[...].reshape(1, 1, d))
        if final:
            y_ref[...] = yn
        else:
            yn_ref[...] = yn.astype(yn_ref.dtype)


def conv_ffn(x3, xn3, prev, w_gate, w_val, val_col0, w_down, layer, conv_w, conv_b, conv_layer,
             out_g, final, bs, lt, tf):
    nseq, L, d = x3.shape
    nlayers, dff, _ = w_down.shape
    assert nseq % bs == 0 and L % lt == 0 and dff % tf == 0 and lt % 8 == 0 and val_col0 % tf == 0
    nj = dff // tf
    voff = val_col0 // tf
    emit_w = w_down.dtype != BF16
    assert not emit_w or (nseq == bs and L == lt)
    halo = CONV_W - 1
    nsplit = 2 if (bs == 1 and lt % 256 == 0) else 1
    row_tile = pl.BlockSpec((bs, lt, d), lambda s, l, j: (s, l, 0))
    tail_tile = pl.BlockSpec((bs, None, halo, tf), lambda s, l, j: (s, l, 0, j))
    y_shape = jax.ShapeDtypeStruct((nseq, L, d), F32)
    tail_shape = jax.ShapeDtypeStruct((nseq, L // lt, halo, dff), F32)
    out_specs = [row_tile] + ([] if final else [row_tile]) + [tail_tile]
    out_shape = [y_shape] + ([] if final else [jax.ShapeDtypeStruct((nseq, L, d), BF16)]) + [tail_shape]
    if emit_w:
        out_specs += [pl.BlockSpec((None, d, tf), lambda s, l, j: (0, 0, j)),
                      pl.BlockSpec((None, d, tf), lambda s, l, j: (0, 0, j)),
                      pl.BlockSpec((None, tf, d), lambda s, l, j: (0, j, 0))]
        out_shape += [jax.ShapeDtypeStruct((1, d, dff), BF16), jax.ShapeDtypeStruct((1, d, dff), BF16),
                      jax.ShapeDtypeStruct((1, dff, d), BF16)]
    n_main = 2 if final else 3
    outs = pl.pallas_call(
        functools.partial(_ffn_kernel, final=final, nsplit=nsplit),
        grid=(nseq // bs, L // lt, nj),
        in_specs=[
            row_tile,
            row_tile,
            pl.BlockSpec((bs, halo, tf), lambda s, l, j: (s, 0, j)),
            pl.BlockSpec((None, d, tf), lambda s, l, j: (layer, 0, j)),
            pl.BlockSpec((None, d, tf), lambda s, l, j: (layer, 0, j + voff)),
            pl.BlockSpec((None, CONV_W, tf), lambda s, l, j: (conv_layer, 0, j)),
            pl.BlockSpec((None, 1, tf), lambda s, l, j: (conv_layer, 0, j)),
            pl.BlockSpec((None, tf, d), lambda s, l, j: (layer, j, 0)),
            pl.BlockSpec((1, d), lambda s, l, j: (0, 0)),
        ],
        out_specs=out_specs,
        out_shape=out_shape,
        scratch_shapes=[
            pltpu.VMEM((nsplit, bs, lt // nsplit + 8, tf), F32),
            pltpu.VMEM((nj, bs, halo, tf), F32),
        ],
        compiler_params=_params("parallel", "arbitrary", "arbitrary"),
        name="conv_ffn",
    )(x3, xn3, prev, w_gate, w_val, conv_w, conv_b.reshape(-1, 1, dff), w_down,
      out_g.reshape(1, d))
    return tuple(outs[:n_main - 1]) + (outs[n_main - 1][:, -1],) + tuple(outs[n_main:])


def _softmax_step(s, m, l, acc, vb):
    m_new = jnp.maximum(m, jnp.max(s, axis=-1, keepdims=True))
    alpha = jnp.exp2(m - m_new)
    p = jnp.exp2(s - m_new)
    l_new = alpha * l + jnp.sum(p, axis=-1, keepdims=True)
    acc_new = alpha * acc + jnp.dot(p.astype(BF16), vb, preferred_element_type=F32)
    return m_new, l_new, acc_new


def _subln(o, g, lam_init):
    return _rms(o, g) * (1.0 - lam_init)


def _attn_prompt_kernel(slopes_ref, lam_ref, q_ref, k_ref, v_ref, g_ref, o_ref,
                        vt_ref, own_ref, p_ref, acc_ref, *, lam_init):
    h = pl.program_id(1)
    qi = pl.program_id(2)
    t = q_ref.shape[0]
    dh = ATTN_HEAD_DIM
    slope2 = slopes_ref[h] * LOG2E
    lam = lam_ref[0]
    q0 = qi * t

    @pl.when(qi == 0)
    def _():
        vt_ref[...] = v_ref[...].T
        k_io = lax.broadcasted_iota(jnp.int32, (t, t), 0)
        q_io = lax.broadcasted_iota(jnp.int32, (t, t), 1)
        own_ref[...] = jnp.where((k_io // CHUNK) <= (q_io // CHUNK),
                                 slope2 * (q_io - jnp.abs(q_io - k_io)).astype(F32), NEG_BIG)

    q = q_ref[...]

    def probs(c, kb, bias, m, l):
        s = lax.dot_general(kb[:, c * dh:(c + 1) * dh], q[:, c * dh:(c + 1) * dh], _NT,
                            preferred_element_type=F32) + bias
        m_new = jnp.maximum(m, jnp.max(s, axis=0, keepdims=True))
        alpha = jnp.exp2(m - m_new)
        p = jnp.exp2(s - m_new)
        return m_new, alpha * l + jnp.sum(p, axis=0, keepdims=True), alpha, p.astype(BF16)

    def add_values(c, k0, alpha, p):
        acc_ref[c] = alpha * acc_ref[c] + jnp.dot(vt_ref[:, pl.ds(k0, t)], p,
                                                  preferred_element_type=F32)

    def tile(k0, prev0, bias, state):
        (m0, l0), (m1, l1, alpha1) = state
        kb = k_ref[pl.ds(k0, t), :]
        m0, l0, alpha0, p0 = probs(0, kb, bias, m0, l0)
        if prev0 is not None:
            add_values(1, prev0, alpha1, p_ref[...])
        m1, l1, alpha1, p1 = probs(1, kb, bias, m1, l1)
        p_ref[...] = p1
        add_values(0, k0, alpha0, p0)
        return (m0, l0), (m1, l1, alpha1)

    neg, zero = jnp.full((1, t), NEG_BIG, F32), jnp.zeros((1, t), F32)
    acc_ref[...] = jnp.zeros(acc_ref.shape, F32)
    own0 = pl.multiple_of(q0, t)
    state = tile(own0, None, own_ref[...], ((neg, zero), (neg, zero, zero)))

    key = lax.broadcasted_iota(jnp.int32, (t, LANES), 0)

    def past(kt, state):
        k0 = pl.multiple_of(kt * t, t)
        prev0 = pl.multiple_of(jnp.where(kt == 0, q0, k0 - t), t)
        bias = slope2 * (key + (k0 - q0)).astype(F32)
        return tile(k0, prev0, jnp.concatenate([bias] * (t // LANES), axis=1), state)

    state = lax.fori_loop(0, qi // 2, lambda i, st: past(2 * i + 1, past(2 * i, st)), state)
    state = lax.cond(qi % 2 == 1, lambda st: past(qi - 1, st), lambda st: st, state)
    (_, l0), (_, l1, alpha1) = state
    add_values(1, pl.multiple_of(jnp.where(qi == 0, q0, q0 - t), t), alpha1, p_ref[...])
    a0, a1 = acc_ref[0], acc_ref[1]
    o = (a0 / l0 - lam * (a1 / l1)).T
    o_ref[...] = _subln(o, g_ref[...], lam_init).astype(o_ref.dtype)


def attn_prompt(q, k, v, slopes, lam, subln, lam_init, tq):
    b, t, _ = q.shape
    hw = 2 * ATTN_HEAD_DIM
    assert t % tq == 0 and tq % CHUNK == 0
    smem = pl.BlockSpec(memory_space=pltpu.SMEM)
    return pl.pallas_call(
        functools.partial(_attn_prompt_kernel, lam_init=lam_init),
        grid=(b, ATTN_HEADS, t // tq),
        in_specs=[
            smem, smem,
            pl.BlockSpec((None, tq, hw), lambda b, h, i: (b, i, h)),
            pl.BlockSpec((None, t, hw), lambda b, h, i: (b, 0, h)),
            pl.BlockSpec((None, t, hw), lambda b, h, i: (b, 0, h)),
            pl.BlockSpec((1, hw), lambda b, h, i: (0, 0)),
        ],
        out_specs=pl.BlockSpec((None, tq, hw), lambda b, h, i: (b, i, h)),
        out_shape=jax.ShapeDtypeStruct(q.shape, BF16),
        scratch_shapes=[
            pltpu.VMEM((hw, t), BF16),
            pltpu.VMEM((tq, tq), F32),
            pltpu.VMEM((tq, tq), BF16),
            pltpu.VMEM((2, hw, tq), F32),
        ],
        compiler_params=_params("parallel", "parallel", "arbitrary"),
        name="attn_prompt",
    )(slopes, lam, q, k, v, subln.reshape(1, hw))


def _attn_sample_kernel(slopes_ref, lam_ref, q_ref, kc_ref, vlo_ref, vhi_ref, kn_ref, vn_ref, g_ref,
                        o_ref, m_ref, l_ref, acc_ref, bias_ref, *, past, lam_init):
    kt = pl.program_id(1)
    nkt = pl.num_programs(1)
    nq = q_ref.shape[0]
    nh, ex = ATTN_HEADS, SAMPLE_EXPAND
    ng = nh // ex
    tk = vlo_ref.shape[0] // nh
    n = ex * tk
    rows = ex * nq
    dh = ATTN_HEAD_DIM
    hw = 2 * dh
    lam = lam_ref[0]

    def q_stack(g, c):
        return jnp.concatenate([q_ref[:, (g + ng * e) * hw + c * dh:(g + ng * e) * hw + (c + 1) * dh]
                                for e in range(ex)], axis=0)

    def scores(g, keys):
        return jnp.concatenate([lax.dot_general(q_stack(g, c), keys[c], _NT,
                                                preferred_element_type=F32) for c in range(2)], axis=0)

    def slope_rows(g):
        e_io = lax.broadcasted_iota(jnp.int32, (2 * rows, 1), 0) % rows // nq
        out = jnp.zeros((2 * rows, 1), F32)
        for e in range(ex):
            out = jnp.where(e_io == e, slopes_ref[g + ng * e] * LOG2E, out)
        return out

    def update(g, s, vb):
        m, l, acc = _softmax_step(s, m_ref[g][:, :1], l_ref[g][:, :1], acc_ref[g], vb)
        m_ref[g] = jnp.broadcast_to(m, m_ref.shape[1:])
        l_ref[g] = jnp.broadcast_to(l, l_ref.shape[1:])
        acc_ref[g] = acc

    @pl.when(kt == 0)
    def _():
        m_ref[...] = jnp.full(m_ref.shape, NEG_BIG, F32)
        l_ref[...] = jnp.zeros(l_ref.shape, F32)
        acc_ref[...] = jnp.zeros(acc_ref.shape, F32)
        e_row = lax.broadcasted_iota(jnp.int32, (2 * rows, n), 0) % rows // nq
        c_io = lax.broadcasted_iota(jnp.int32, (2 * rows, n), 1)
        for g in range(ng):
            bias_ref[g] = jnp.where(e_row == c_io % ex,
                                    slope_rows(g) * (c_io // ex - past).astype(F32), NEG_BIG)

    for g in range(ng):
        keys = [kc_ref[pl.ds(2 * g + c, n, stride=2 * ng), :].astype(BF16) for c in range(2)]
        vals = jnp.concatenate([vlo_ref[pl.ds(g, n, stride=ng), :], vhi_ref[pl.ds(g, n, stride=ng), :]],
                               axis=1).astype(BF16)
        update(g, scores(g, keys) + (bias_ref[g] + slope_rows(g) * (kt * tk).astype(F32)), vals)

    @pl.when(kt == nkt - 1)
    def _():
        r_io = lax.broadcasted_iota(jnp.int32, (2 * rows, rows), 0) % rows
        c_io = lax.broadcasted_iota(jnp.int32, (2 * rows, rows), 1)
        tok, key = r_io % nq, c_io % nq
        rel_n = (tok - jnp.abs(tok - key)).astype(F32)
        allowed = (r_io // nq == c_io // nq) & (((key + past) // CHUNK) <= ((tok + past) // CHUNK))
        for g in range(ng):
            heads = [g + ng * e for e in range(ex)]
            keys = [jnp.concatenate([kn_ref[:, h * hw + c * dh:h * hw + (c + 1) * dh] for h in heads],
                                    axis=0) for c in range(2)]
            vals = jnp.concatenate([vn_ref[:, h * hw:(h + 1) * hw] for h in heads], axis=0)
            update(g, jnp.where(allowed, scores(g, keys) + slope_rows(g) * rel_n, NEG_BIG), vals)
            on = acc_ref[g] / l_ref[g][:, :1]
            for e, h in enumerate(heads):
                o = on[e * nq:(e + 1) * nq] - lam * on[rows + e * nq:rows + (e + 1) * nq]
                o_ref[:, h * hw:(h + 1) * hw] = _subln(o, g_ref[...], lam_init).astype(o_ref.dtype)


def attn_sample(q, k_new, v_new, k_cache, v_cache, slopes, lam, subln, lam_init, tk):
    b, nq, d = q.shape
    nh, dh = ATTN_HEADS, ATTN_HEAD_DIM
    past = v_cache.shape[1] // nh
    hw = 2 * dh
    ex = SAMPLE_EXPAND
    ng = nh // ex
    assert past % tk == 0 and past > 0 and dh == LANES and nh % ex == 0
    smem = pl.BlockSpec(memory_space=pltpu.SMEM)
    return pl.pallas_call(
        functools.partial(_attn_sample_kernel, past=past, lam_init=lam_init),
        grid=(b, past // tk),
        in_specs=[
            smem, smem,
            pl.BlockSpec((None, nq, d), lambda b, k: (b, 0, 0)),
            pl.BlockSpec((None, tk * nh * 2, dh), lambda b, k: (b, k, 0)),
            pl.BlockSpec((None, tk * nh, dh), lambda b, k: (b, k, 0)),
            pl.BlockSpec((None, tk * nh, dh), lambda b, k: (b, k, 1)),
            pl.BlockSpec((None, nq, d), lambda b, k: (b, 0, 0)),
            pl.BlockSpec((None, nq, d), lambda b, k: (b, 0, 0)),
            pl.BlockSpec((1, hw), lambda b, k: (0, 0)),
        ],
        out_specs=pl.BlockSpec((None, nq, d), lambda b, k: (b, 0, 0)),
        out_shape=jax.ShapeDtypeStruct(q.shape, BF16),
        scratch_shapes=[
            pltpu.VMEM((ng, 2 * ex * nq, LANES), F32),
            pltpu.VMEM((ng, 2 * ex * nq, LANES), F32),
            pltpu.VMEM((ng, 2 * ex * nq, hw), F32),
            pltpu.VMEM((ng, 2 * ex * nq, ex * tk), F32),
        ],
        compiler_params=_params("parallel", "arbitrary"),
        name="attn_sample",
    )(slopes, lam, q, k_cache, v_cache, v_cache, k_new, v_new, subln.reshape(1, hw))


def _cumsum_rows(tri, x):
    hi = x.astype(BF16)
    r1 = x - hi.astype(F32)
    mid = r1.astype(BF16)
    lo = (r1 - mid.astype(F32)).astype(BF16)
    n = x.shape[1]
    y = jnp.dot(tri, jnp.concatenate([hi, mid, lo], axis=1), preferred_element_type=F32)
    return y[:, :n] + y[:, n:2 * n] + y[:, 2 * n:]


def _hgrn_kernel(q_ref, k_ref, i_ref, g_ref, gn_ref, s0_ref, o_ref, sout_ref,
                 s_ref, qb_ref, kb_ref, *, chunk, sub):
    tb = pl.program_id(2)
    hb, tbs, dk = q_ref.shape
    nch = tbs // chunk
    nsub = chunk // sub

    @pl.when(tb == 0)
    def _():
        s_ref[...] = s0_ref[...]
        qb_ref[...] = jnp.zeros(qb_ref.shape, BF16)
        kb_ref[...] = jnp.zeros(kb_ref.shape, BF16)

    r_io = lax.broadcasted_iota(jnp.int32, (chunk, chunk), 0)
    c_io = lax.broadcasted_iota(jnp.int32, (chunk, chunk), 1)
    causal = c_io <= r_io
    tri = causal.astype(BF16)

    for h in range(hb):
        parts = []
        for c in range(nch):
            t0, slot = c * chunk, h * nch + c
            kk = k_ref[h, t0:t0 + chunk, :]
            qs = q_ref[h, t0:t0 + chunk, :]
            v = i_ref[h, t0:t0 + chunk, :].astype(BF16)
            b = _cumsum_rows(tri, jnp.log(1.0 - kk))
            rows = [slice(sub * j, sub * (j + 1)) for j in range(nsub)]
            ends = [b[sub * (j + 1) - 1:sub * (j + 1), :] for j in range(nsub)]
            b_last = ends[-1]
            k_own = [kk[rows[j]] * jnp.exp(ends[j] - b[rows[j]]) for j in range(nsub)]
            q_end, k_end = [], []
            for i in range(nsub):
                lanes = slice(dk * i, dk * (i + 1))
                ref = ends[i - 1] if i > 0 else jnp.zeros((1, dk), F32)
                q_rel = qs[rows[i]] * jnp.exp(b[rows[i]] - ref)
                qb_ref[slot, rows[i], lanes] = q_rel.astype(BF16)
                q_end.append(q_rel * jnp.exp(ref))
                for j in range(i):
                    k_rel = k_own[j] if j == i - 1 else k_own[j] * jnp.exp(ref - ends[j])
                    kb_ref[slot, rows[j], lanes] = k_rel.astype(BF16)
                kb_ref[slot, rows[i], lanes] = (
                    kk[rows[i]] * jnp.exp(jnp.minimum(ref - b[rows[i]], EXP_CLAMP))).astype(BF16)
                k_end.append(k_own[i] if i == nsub - 1 else k_own[i] * jnp.exp(b_last - ends[i]))
            a = lax.dot_general(qb_ref[slot], kb_ref[slot], _NT, preferred_element_type=F32)
            a = jnp.where(causal, a, 0.0).astype(BF16)
            q_end = jnp.concatenate(q_end, axis=0).astype(BF16)
            k_end = jnp.concatenate(k_end, axis=0).astype(BF16)
            upd = lax.dot_general(k_end, v, _TN, preferred_element_type=F32)
            decay = jnp.transpose(jnp.broadcast_to(jnp.exp(b_last), (REC_DV, dk)))
            parts.append((jnp.concatenate([a, q_end], axis=1), v, decay, upd))
        state = s_ref[h]
        for c in range(nch):
            t0 = c * chunk
            a_q, v, decay, upd = parts[c]
            o = jnp.dot(a_q, jnp.concatenate([v, state.astype(BF16)], axis=0),
                        preferred_element_type=F32)
            state = decay * state + upd
            y = _rms(o, gn_ref[...]) * g_ref[h, t0:t0 + chunk, :]
            o_ref[t0:t0 + chunk, h * REC_DV:(h + 1) * REC_DV] = y.astype(o_ref.dtype)
        s_ref[h] = state
    sout_ref[...] = s_ref[...]


def hgrn(qfig, s0, out_norm, hb, tbs, chunk, sub):
    _, nh, b, t, dk = qfig.shape
    dv = REC_DV
    assert t % tbs == 0 and tbs % chunk == 0 and chunk % sub == 0 and sub % 16 == 0 and nh % hb == 0
    nsub = chunk // sub
    nslot = hb * (tbs // chunk)

    def kind(k):
        return pl.BlockSpec((None, hb, None, tbs, dk), lambda b, h, i: (k, h, b, i, 0))

    return pl.pallas_call(
        functools.partial(_hgrn_kernel, chunk=chunk, sub=sub),
        grid=(b, nh // hb, t // tbs),
        in_specs=[
            kind(0), kind(1), kind(2), kind(3),
            pl.BlockSpec((1, dv), lambda b, h, i: (0, 0)),
            pl.BlockSpec((None, hb, dk, dv), lambda b, h, i: (b, h, 0, 0)),
        ],
        out_specs=[
            pl.BlockSpec((None, tbs, hb * dv), lambda b, h, i: (b, i, h)),
            pl.BlockSpec((None, hb, dk, dv), lambda b, h, i: (b, h, 0, 0)),
        ],
        out_shape=[
            jax.ShapeDtypeStruct((b, t, nh * dv), BF16),
            jax.ShapeDtypeStruct((b, nh, dk, dv), F32),
        ],
        scratch_shapes=[
            pltpu.VMEM((hb, dk, dv), F32),
            pltpu.VMEM((nslot, chunk, nsub * dk), BF16),
            pltpu.VMEM((nslot, chunk, nsub * dk), BF16),
        ],
        compiler_params=_params("parallel", "parallel", "arbitrary"),
        name="hgrn",
    )(qfig, qfig, qfig, qfig, out_norm.reshape(1, dv), s0)


def _pick(n, candidates):
    for c in candidates:
        if n % c == 0:
            return c
    return n


def _trunk(x3, is_prompt, cache_k, cache_v, state_hgrn, state_conv, p, ffn_weights, w_qfig):
    nseq, L, d = x3.shape
    m = nseq * L
    dff = p["ffn_conv_b"].shape[1]
    tm = _pick(m, (1024, 512, 256))
    tf = _pick(dff, (512, 256, 128))
    if is_prompt:
        ffn_bs, ffn_lt = 1, _pick(L, (512, 256, 128))
    else:
        ffn_bs, ffn_lt = nseq, L
    x = x3.reshape(m, d)

    wqkv = p["attn_w_qkv"][0]
    nqk = ATTN_HEADS * 2 * ATTN_HEAD_DIM
    g0 = p["mixer_norm"][0]
    tmw = _pick(m, (512, 256))
    q, xn = norm_matmul(x, g0, wqkv, 0, nqk, ((BF16, "rows"), (BF16, "xn")), tmw, nqk,
                        out_scale=ATTN_SCALE * LOG2E)
    k32, k16 = norm_matmul(xn, None, wqkv, nqk, nqk, ((F32, "interleaved"), (BF16, "rows")), tmw, nqk)
    v32, v16 = norm_matmul(xn, None, wqkv, 2 * nqk, nqk, ((F32, "rows"), (BF16, "rows")), tmw, nqk)
    sh = (nseq, L, nqk)
    lam_init = 0.8 - 0.6 * math.exp(-0.3 * 0)
    if is_prompt:
        o = attn_prompt(q.reshape(sh), k16.reshape(sh), v16.reshape(sh), p["slopes"], p["lam"],
                        p["attn_subln"][0], lam_init, tq=_pick(L, (512, 256, 128, 64)))
    else:
        past = cache_k.shape[2]
        o = attn_sample(q.reshape(sh), k16.reshape(sh), v16.reshape(sh),
                        cache_k[0].reshape(nseq, past * ATTN_HEADS * 2, ATTN_HEAD_DIM),
                        cache_v[0].reshape(nseq, past * ATTN_HEADS, ATTN_V_DIM),
                        p["slopes"], p["lam"], p["attn_subln"][0], lam_init,
                        tk=_pick(past, (1024, 512, 256, 128)))
    x, xn = matmul_res(o.reshape(m, nqk), p["attn_w_o"][0], x, p["ffn_norm"][0], tmw)
    prev0 = jnp.zeros((nseq, CONV_W - 1, dff), F32) if is_prompt else state_conv[0]
    def ffn(i, x, xn, prev, out_g, final):
        w_gate, w_val, val_col0, w_down, layer = ffn_weights[i]
        return conv_ffn(x.reshape(nseq, L, d), xn.reshape(nseq, L, d), prev, w_gate, w_val, val_col0,
                        w_down, layer, p["ffn_conv_w"], p["ffn_conv_b"], i, out_g, final,
                        ffn_bs, ffn_lt, tf)

    x3b, xn3, cs0, *w16_0 = ffn(0, x, xn, prev0, p["mixer_norm"][1], False)
    x = x3b.reshape(m, d)

    wq = w_qfig
    wq_f32 = wq.dtype == F32
    qfig, *wq16 = norm_matmul(xn3.reshape(m, d), None, wq, 0, wq.shape[1],
                              ((F32, "groups"),) + (((BF16, "w16"),) if wq_f32 else ()),
                              tm, 512 if wq_f32 else 1024, hgrn_lb=p["lb"])
    if is_prompt:
        s0 = jnp.zeros((nseq, REC_HEADS, REC_DK, REC_DV), F32)
        chunk = _pick(L, (128, 64, 32, 16))
        hb, tbs, sub = 2, _pick(L, (1024, 512, 256, 128, 64, 32, 16)), 16
    else:
        s0 = state_hgrn[0]
        hb, chunk, tbs, sub = REC_HEADS, L, L, 16
    o, s_fin = hgrn(qfig.reshape(4, REC_HEADS, nseq, L, REC_DK), s0, p["rec_out_norm"][0],
                    hb, tbs, chunk, sub)
    x, xn = matmul_res(o.reshape(m, d), p["rec_w_o"][0], x, p["ffn_norm"][1], tmw)
    prev1 = jnp.zeros((nseq, CONV_W - 1, dff), F32) if is_prompt else state_conv[1]
    y, cs1, *w16_1 = ffn(1, x, xn, prev1, p["final_norm"], True)
    kshape = (1, nseq, L, ATTN_HEADS, 2, ATTN_HEAD_DIM)
    vshape = (1, nseq, L, ATTN_HEADS, ATTN_V_DIM)
    outs = (y, k32.reshape(kshape), v32.reshape(vshape), s_fin[None], jnp.stack([cs0, cs1]))
    w16 = [(w[0], w[1], 0, w[2], 0) for w in (w16_0, w16_1) if w]
    return outs, w16, (wq16[0] if wq16 else None)


def kernel(x_prompt, x_sample, cache_k, cache_v, state_hgrn, state_conv, mixer_norm, ffn_norm,
           attn_w_qkv, attn_lambda_q1, attn_lambda_k1, attn_lambda_q2, attn_lambda_k2, attn_subln,
           attn_w_o, rec_w_qfig, rec_lower_bounds, rec_out_norm, rec_w_o, ffn_w_up, ffn_conv_w,
           ffn_conv_b, ffn_w_down, final_norm):
    lam_init = 0.8 - 0.6 * math.exp(-0.3 * 0)
    lam = (jnp.exp(jnp.sum(attn_lambda_q1[0] * attn_lambda_k1[0]))
           - jnp.exp(jnp.sum(attn_lambda_q2[0] * attn_lambda_k2[0])) + lam_init).reshape(1)
    slopes = jnp.exp2(-8.0 * jnp.arange(1, ATTN_HEADS + 1, dtype=F32) / ATTN_HEADS)
    lbs = jax.nn.softmax(rec_lower_bounds.astype(F32), axis=0)
    lb = (jnp.cumsum(lbs, axis=0) - lbs[0])[1]
    p = dict(
        mixer_norm=mixer_norm, ffn_norm=ffn_norm, attn_subln=attn_subln, rec_out_norm=rec_out_norm,
        ffn_conv_w=ffn_conv_w, ffn_conv_b=ffn_conv_b, final_norm=final_norm,
        lam=lam, slopes=slopes, lb=lb,
        attn_w_qkv=attn_w_qkv.astype(BF16), attn_w_o=attn_w_o.astype(BF16),
        rec_w_o=rec_w_o.astype(BF16),
    )
    dff = ffn_w_down.shape[1]
    ffn32 = [(ffn_w_up, ffn_w_up, dff, ffn_w_down, i) for i in range(ffn_w_down.shape[0])]
    (ys, ks, vs, ss, cs), ffn16, wq16 = _trunk(x_sample, False, cache_k, cache_v, state_hgrn,
                                               state_conv, p, ffn32, rec_w_qfig[0])
    (yp, kp, vp, sp, cp), _, _ = _trunk(x_prompt, True, None, None, None, None, p, ffn16, wq16)
    return (yp, ys, kp, vp, ks, vs, sp, ss, cp, cs)
```
